```python
import jax, jax.numpy as jnp
from jax import lax
import numpy as np

D_MODEL = 4096
BATCH = 2
SEQ = 8192
DEPTH = 2

GRID_W = 64
CTX_LEN = 256
F_DIM = 1024
N_HEADS = 16
Q_LORA = 768
KV_LORA = 512
QK_NOPE = 128
QK_ROPE = 64
V_DIM = 128
QK_DIM = QK_NOPE + QK_ROPE
MLA_DIM = N_HEADS * V_DIM
ROPE_BASE = 10000.0
Q_BLOCK = 128
C_DIM = 1024
CONV_W = 31
N_BRANCH = 3
EPS = 1e-6

OFF_F = 0
OFF_F_GATE = OFF_F + F_DIM
OFF_Q = OFF_F_GATE + F_DIM
OFF_KV = OFF_Q + Q_LORA
OFF_KROPE = OFF_KV + KV_LORA
OFF_M_GATE = OFF_KROPE + QK_ROPE
OFF_GLU = OFF_M_GATE + MLA_DIM
OFF_C_GATE = OFF_GLU + 2 * C_DIM
OFF_MERGE = OFF_C_GATE + C_DIM
N_IN = OFF_MERGE + N_BRANCH * D_MODEL

kernel_name = 'hybrid_fourier_mla_conformer_dit_block'


def rms_norm(x, g):
    xf = x.astype(jnp.float32)
    y = xf * lax.rsqrt(jnp.mean(xf * xf, axis=-1, keepdims=True) + EPS)
    return (y * g.astype(jnp.float32)).astype(x.dtype)


def layer_norm(x, g, b):
    xf = x.astype(jnp.float32)
    mu = jnp.mean(xf, axis=-1, keepdims=True)
    d = xf - mu
    var = jnp.mean(d * d, axis=-1, keepdims=True)
    return (d * lax.rsqrt(var + EPS) * g.astype(jnp.float32) + b.astype(jnp.float32)).astype(x.dtype)


def rope_half(x, pos):
    half = x.shape[-1] // 2
    inv = ROPE_BASE ** (-jnp.arange(half, dtype=jnp.float32) / half)
    ang = pos.astype(jnp.float32)[:, None] * inv[None, :]
    cos = jnp.cos(ang)[:, None, :].astype(x.dtype)
    sin = jnp.sin(ang)[:, None, :].astype(x.dtype)
    x1, x2 = x[..., :half], x[..., half:]
    return jnp.concatenate([x1 * cos - x2 * sin, x1 * sin + x2 * cos], axis=-1)


def axial_rope(x, rows, cols):
    h = x.shape[-1] // 2
    return jnp.concatenate([rope_half(x[..., :h], rows), rope_half(x[..., h:], cols)], axis=-1)


def mla_queries(cq, q_a_norm, w_uq, q_norm, rows, cols):
    b, s, _ = cq.shape
    q = (rms_norm(cq, q_a_norm) @ w_uq).reshape(b, s, N_HEADS, QK_DIM)
    q = rms_norm(q, q_norm)
    if rows is None:
        return q
    return jnp.concatenate([q[..., :QK_NOPE], axial_rope(q[..., QK_NOPE:], rows, cols)], axis=-1)


def mla_keys_values(ckv, k_rope, kv_a_norm, w_ukv, k_norm, rows, cols):
    b, s, _ = ckv.shape
    kv = (rms_norm(ckv, kv_a_norm) @ w_ukv).reshape(b, s, N_HEADS, QK_NOPE + V_DIM)
    k_nope, v = kv[..., :QK_NOPE], kv[..., QK_NOPE:]
    k_r = jnp.broadcast_to(k_rope[:, :, None, :], (b, s, N_HEADS, QK_ROPE))
    k = rms_norm(jnp.concatenate([k_nope, k_r], axis=-1), k_norm)
    if rows is not None:
        k = jnp.concatenate([k[..., :QK_NOPE], axial_rope(k[..., QK_NOPE:], rows, cols)], axis=-1)
    return k, v


def attend(q, k, v):
    s = jnp.einsum('bqhd,bkhd->bhqk', q, k).astype(jnp.float32) * (QK_DIM ** -0.5)
    p = jax.nn.softmax(s, axis=-1).astype(v.dtype)
    return jnp.einsum('bhqk,bkhd->bqhd', p, v)


def blocked_attend(q, k, v):
    b, s, h, dh = q.shape
    nb = s // Q_BLOCK
    qb = q.reshape(b, nb, Q_BLOCK, h, dh).transpose(1, 0, 2, 3, 4)
    ob = lax.map(lambda blk: attend(blk, k, v), qb)
    return ob.transpose(1, 0, 2, 3, 4).reshape(b, s, h * V_DIM)


def fourier_mix(u, w_fnet):
    z = jnp.fft.fft2(u.astype(jnp.float32), axes=(1, 2), norm='ortho').real.astype(u.dtype)
    return z @ w_fnet


def conformer_conv(glu_in, conv_w, conv_b, cln_g, cln_b, w_pw2):
    a, g = jnp.split(glu_in, 2, axis=-1)
    u = a * jax.nn.sigmoid(g)
    u = lax.conv_general_dilated(u, conv_w[:, None, :], window_strides=(1,),
                                 padding=((CONV_W // 2, CONV_W // 2),),
                                 dimension_numbers=('NWC', 'WIO', 'NWC'),
                                 feature_group_count=C_DIM) + conv_b
    u = jax.nn.silu(layer_norm(u, cln_g, cln_b))
    return u @ w_pw2


def mix_and_merge(z, attn, w_fnet, conv_w, conv_b, cln_g, cln_b, w_pw2, w_br_f, w_br_m, w_br_c, w_out):
    y_f = fourier_mix(z[..., OFF_F:OFF_F_GATE], w_fnet) * jax.nn.silu(z[..., OFF_F_GATE:OFF_Q])
    y_m = attn * jax.nn.silu(z[..., OFF_M_GATE:OFF_GLU])
    y_c = conformer_conv(z[..., OFF_GLU:OFF_C_GATE], conv_w, conv_b, cln_g, cln_b, w_pw2) \
        * jax.nn.silu(z[..., OFF_C_GATE:OFF_MERGE])
    g_f, g_m, g_c = jnp.split(jax.nn.sigmoid(z[..., OFF_MERGE:]), N_BRANCH, axis=-1)
    merged = g_f * (y_f @ w_br_f) + g_m * (y_m @ w_br_m) + g_c * (y_c @ w_br_c)
    return merged @ w_out


def setup_inputs(seed: int = 0) -> dict:
    key = jax.random.key(seed)
    ks = jax.random.split(key, 24)
    L, D = DEPTH, D_MODEL
    nrm = lambda k, shape, scale: jax.random.normal(k, shape, dtype=jnp.float32) * scale
    gain = lambda k, shape: 1.0 + 0.05 * jax.random.normal(k, shape, dtype=jnp.float32)
    return {
        'x': nrm(ks[0], (BATCH, SEQ, D), 1.0),
        'c': nrm(ks[1], (BATCH, D), 1.0),
        'ctx': nrm(ks[2], (BATCH, CTX_LEN, D), 1.0),
        'c_ctx': nrm(ks[3], (D,), 1.0),
        'norm_g': gain(ks[4], (L, D)),
        'w_ada': nrm(ks[5], (L, D, 3 * D), 0.5 * D ** -0.5),
        'b_ada': nrm(ks[6], (L, 3 * D), 0.02),
        'w_in': nrm(ks[7], (L, D, N_IN), D ** -0.5),
        'q_a_norm': gain(ks[8], (L, Q_LORA)),
        'w_uq': nrm(ks[9], (L, Q_LORA, N_HEADS * QK_DIM), Q_LORA ** -0.5),
        'kv_a_norm': gain(ks[10], (L, KV_LORA)),
        'w_ukv': nrm(ks[11], (L, KV_LORA, N_HEADS * (QK_NOPE + V_DIM)), KV_LORA ** -0.5),
        'q_norm': gain(ks[12], (L, QK_DIM)),
        'k_norm': gain(ks[13], (L, QK_DIM)),
        'w_fnet': nrm(ks[14], (L, F_DIM, F_DIM), F_DIM ** -0.5),
        'conv_w': nrm(ks[15], (L, CONV_W, C_DIM), CONV_W ** -0.5),
        'conv_b': nrm(ks[16], (L, C_DIM), 0.02),
        'cln_g': gain(ks[17], (L, C_DIM)),
        'cln_b': nrm(ks[18], (L, C_DIM), 0.02),
        'w_pw2': nrm(ks[19], (L, C_DIM, C_DIM), C_DIM ** -0.5),
        'w_br_f': nrm(ks[20], (L, F_DIM, D), F_DIM ** -0.5),
        'w_br_m': nrm(ks[21], (L, MLA_DIM, D), MLA_DIM ** -0.5),
        'w_br_c': nrm(ks[22], (L, C_DIM, D), C_DIM ** -0.5),
        'w_out': nrm(ks[23], (L, D, D), D ** -0.5),
    }


def reference(x, c, ctx, c_ctx, norm_g, w_ada, b_ada, w_in, q_a_norm, w_uq, kv_a_norm, w_ukv,
              q_norm, k_norm, w_fnet, conv_w, conv_b, cln_g, cln_b, w_pw2, w_br_f, w_br_m,
              w_br_c, w_out):
    b, n_tok, _ = x.shape
    n_ctx = ctx.shape[1]
    n_rows = n_tok // GRID_W
    rows = jnp.repeat(jnp.arange(n_rows, dtype=jnp.int32), GRID_W)
    cols = jnp.tile(jnp.arange(GRID_W, dtype=jnp.int32), n_rows)
    xl, xc = x, ctx
    for l in range(DEPTH):
        last = l == DEPTH - 1
        shift_l, scale_l, gate_l = jnp.split(jax.nn.silu(c) @ w_ada[l] + b_ada[l], 3, axis=-1)
        n_mod_c = 2 if last else 3
        mod_c = jax.nn.silu(c_ctx) @ w_ada[l][:, :n_mod_c * D_MODEL] + b_ada[l][:n_mod_c * D_MODEL]
        shift_c, scale_c = mod_c[:D_MODEL], mod_c[D_MODEL:2 * D_MODEL]
        hl = rms_norm(xl, norm_g[l]) * (1 + scale_l[:, None, :]) + shift_l[:, None, :]
        hc = rms_norm(xc, norm_g[l]) * (1 + scale_c) + shift_c
        zl = hl @ w_in[l]
        if last:
            zkv_c = hc @ w_in[l][:, OFF_KV:OFF_M_GATE]
        else:
            zc = hc @ w_in[l]
            zkv_c = zc[..., OFF_KV:OFF_M_GATE]
        kc, vc = mla_keys_values(zkv_c[..., :KV_LORA], zkv_c[..., KV_LORA:], kv_a_norm[l],
                                 w_ukv[l], k_norm[l], None, None)
        kl, vl = mla_keys_values(zl[..., OFF_KV:OFF_KROPE], zl[..., OFF_KROPE:OFF_M_GATE],
                                 kv_a_norm[l], w_ukv[l], k_norm[l], rows, cols)
        ql = mla_queries(zl[..., OFF_Q:OFF_KV], q_a_norm[l], w_uq[l], q_norm[l], rows, cols)
        ol = blocked_attend(ql, jnp.concatenate([kc, kl], axis=1), jnp.concatenate([vc, vl], axis=1))
        branch_w = (w_fnet[l], conv_w[l], conv_b[l], cln_g[l], cln_b[l], w_pw2[l],
                    w_br_f[l], w_br_m[l], w_br_c[l], w_out[l])
        new_xl = xl + gate_l[:, None, :] * mix_and_merge(zl, ol, *branch_w)
        if not last:
            qc = mla_queries(zc[..., OFF_Q:OFF_KV], q_a_norm[l], w_uq[l], q_norm[l], None, None)
            oc = attend(qc, kc, vc).reshape(b, n_ctx, MLA_DIM)
            xc = xc + mod_c[2 * D_MODEL:] * mix_and_merge(zc, oc, *branch_w)
        xl = new_xl
    return xl
```

```python
import functools
import math

import numpy as np
import jax
import jax.numpy as jnp
from jax import lax
from jax.experimental import pallas as pl
from jax.experimental.pallas import tpu as pltpu

F32 = jnp.float32
BF16 = jnp.bfloat16

EPS = 1e-6
QK_NOPE = 128
QK_ROPE = 64
V_DIM = 128
QK_DIM = QK_NOPE + QK_ROPE
HEAD_PAD = 256
LANE = 128
GRID_W = 64
CONV_W = 31
CONV_HALO = 16
ROPE_BASE = 10000.0
N_BRANCH = 3
VMEM_LIMIT = 52 * 1024 * 1024


def _cparams(sem):
    return pltpu.CompilerParams(dimension_semantics=sem, vmem_limit_bytes=VMEM_LIMIT)


def _tile(n, pref, align=8):
    if n <= pref:
        return n
    t = (pref // align) * align
    while t >= align:
        if n % t == 0:
            return t
        t -= align
    return n


def _silu(v):
    return v * jax.nn.sigmoid(v)


def _split_hi_lo(v):
    hi = v.astype(BF16)
    lo = (v - hi.astype(F32)).astype(BF16)
    return hi, lo


def _dot(a, b):
    return jnp.dot(a, b, preferred_element_type=F32)


def _dot3(a_hi, a_lo, b_hi, b_lo):
    return _dot(a_hi, b_hi) + (_dot(a_hi, b_lo) + _dot(a_lo, b_hi))


def _mm_body(*refs, nk, n_extra, epi, a_act):
    a_ref, b_ref = refs[0], refs[1]
    extra = refs[2:2 + n_extra]
    o_ref = refs[2 + n_extra]
    a = a_ref[...]
    if a_act is not None:
        a = a_act(a.astype(F32))
    p = _dot(a.astype(BF16), b_ref[...].astype(BF16))
    if nk == 1:
        o_ref[...] = epi(p, *[e[...] for e in extra]).astype(o_ref.dtype)
    else:
        acc_ref = refs[3 + n_extra]
        k = pl.program_id(2)

        @pl.when(k == 0)
        def _():
            acc_ref[...] = p

        @pl.when(k > 0)
        def _():
            acc_ref[...] += p

        @pl.when(k == nk - 1)
        def _():
            o_ref[...] = epi(acc_ref[...], *[e[...] for e in extra]).astype(o_ref.dtype)


def _mm(a, b, *, out_dtype, tm=1024, tn=512, tk=4096, epi=None, extras=(), a_act=None,
        rows_per_batch=None, name="mm"):
    m, kdim = a.shape
    n = b.shape[1]
    tm = _tile(m, tm)
    if rows_per_batch is not None:
        tm = _tile(rows_per_batch, tm)
    tn = _tile(n, tn, LANE)
    tk = _tile(kdim, tk, LANE)
    nk = kdim // tk
    if epi is None:
        epi = lambda p: p
    in_specs = [pl.BlockSpec((tm, tk), lambda i, j, k: (i, k)),
                pl.BlockSpec((tk, tn), lambda i, j, k: (k, j))]
    args = [a, b]
    for kind, arr in extras:
        if kind == "mn":
            in_specs.append(pl.BlockSpec((tm, tn), lambda i, j, k: (i, j)))
        elif kind == "n":
            in_specs.append(pl.BlockSpec((1, tn), lambda i, j, k: (0, j)))
        else:
            bpt = rows_per_batch // tm
            in_specs.append(pl.BlockSpec((None, 1, tn), lambda i, j, k, bpt=bpt: (i // bpt, 0, j)))
        args.append(arr)
    scratch = [pltpu.VMEM((tm, tn), F32)] if nk > 1 else []
    return pl.pallas_call(
        functools.partial(_mm_body, nk=nk, n_extra=len(extras), epi=epi, a_act=a_act),
        grid=(m // tm, n // tn, nk),
        in_specs=in_specs,
        out_specs=pl.BlockSpec((tm, tn), lambda i, j, k: (i, j)),
        out_shape=jax.ShapeDtypeStruct((m, n), out_dtype),
        scratch_shapes=scratch,
        compiler_params=_cparams(("parallel", "parallel", "arbitrary")),
        name=name,
    )(*args)


def _norm_body(x_ref, g_ref, scale_ref, shift_ref, o_ref):
    x = x_ref[...]
    r = lax.rsqrt(jnp.mean(x * x, axis=-1, keepdims=True) + EPS)
    y = x * r * g_ref[...]
    o_ref[...] = (y * (1.0 + scale_ref[...]) + shift_ref[...]).astype(o_ref.dtype)


def _norm_mod(x, g, scale, shift):
    b, s, d = x.shape
    tr = _tile(s, 256)
    return pl.pallas_call(
        _norm_body,
        grid=(b, s // tr),
        in_specs=[pl.BlockSpec((None, tr, d), lambda bi, i: (bi, i, 0)),
                  pl.BlockSpec((1, d), lambda bi, i: (0, 0)),
                  pl.BlockSpec((None, 1, d), lambda bi, i: (bi, 0, 0)),
                  pl.BlockSpec((None, 1, d), lambda bi, i: (bi, 0, 0))],
        out_specs=pl.BlockSpec((None, tr, d), lambda bi, i: (bi, i, 0)),
        out_shape=jax.ShapeDtypeStruct((b, s, d), BF16),
        compiler_params=_cparams(("parallel", "parallel")),
        name="norm_mod",
    )(x, g, scale, shift)


def _qprep_body(z_ref, an_ref, w_ref, gn_ref, gr_ref, gs_ref, cos_ref, sin_ref, q_ref, *, n_heads, scale):
    x = z_ref[...]
    r = lax.rsqrt(jnp.mean(x * x, axis=-1, keepdims=True) + EPS)
    xn = (x * r * an_ref[...]).astype(BF16)
    acc = _dot(xn, w_ref[...])
    cos = cos_ref[...] * gr_ref[...]
    sin = sin_ref[...] * gs_ref[...]
    gn = gn_ref[...]
    hw = n_heads * LANE
    for h in range(n_heads):
        nope = acc[:, h * LANE:(h + 1) * LANE]
        rp = acc[:, hw + h * LANE: hw + (h + 1) * LANE]
        rs = acc[:, 2 * hw + h * LANE: 2 * hw + (h + 1) * LANE]
        ss = jnp.sum(nope * nope, axis=-1, keepdims=True) + jnp.sum(rp * rp, axis=-1, keepdims=True)
        inv = lax.rsqrt(ss * (1.0 / QK_DIM) + EPS) * scale
        q_ref[h, :, 0:LANE] = (nope * inv * gn).astype(q_ref.dtype)
        q_ref[h, :, LANE:HEAD_PAD] = ((rp * cos + rs * sin) * inv).astype(q_ref.dtype)


def _qprep(zq, a_norm, w_p, gn, gr, gs, cos, sin, n_heads):
    b, s, ql = zq.shape
    tr = _tile(s, 256)
    nw = w_p.shape[1]
    return pl.pallas_call(
        functools.partial(_qprep_body, n_heads=n_heads, scale=QK_DIM ** -0.5),
        grid=(b, s // tr),
        in_specs=[pl.BlockSpec((None, tr, ql), lambda bi, i: (bi, i, 0)),
                  pl.BlockSpec((1, ql), lambda bi, i: (0, 0)),
                  pl.BlockSpec((ql, nw), lambda bi, i: (0, 0)),
                  pl.BlockSpec((1, LANE), lambda bi, i: (0, 0)),
                  pl.BlockSpec((1, LANE), lambda bi, i: (0, 0)),
                  pl.BlockSpec((1, LANE), lambda bi, i: (0, 0)),
                  pl.BlockSpec((tr, LANE), lambda bi, i: (i, 0)),
                  pl.BlockSpec((tr, LANE), lambda bi, i: (i, 0))],
        out_specs=pl.BlockSpec((None, n_heads, tr, HEAD_PAD), lambda bi, i: (bi, 0, i, 0)),
        out_shape=jax.ShapeDtypeStruct((b, n_heads, s, HEAD_PAD), BF16),
        compiler_params=_cparams(("parallel", "parallel")),
        name="q_prep",
    )(zq, a_norm, w_p, gn, gr, gs, cos, sin)


def _kvprep_body(z_ref, an_ref, w_ref, gn_ref, gr_ref, gs_ref, cos_ref, sin_ref, k_ref, v_ref, *,
                 n_heads, kv_lora):
    x = z_ref[:, 0:kv_lora]
    kr = z_ref[:, kv_lora:kv_lora + LANE]
    krs = z_ref[:, kv_lora + LANE:kv_lora + 2 * LANE]
    r = lax.rsqrt(jnp.mean(x * x, axis=-1, keepdims=True) + EPS)
    xn = (x * r * an_ref[...]).astype(BF16)
    acc = _dot(xn, w_ref[...])
    ss_r = jnp.sum(kr * kr, axis=-1, keepdims=True)
    rope = kr * (cos_ref[...] * gr_ref[...]) + krs * (sin_ref[...] * gs_ref[...])
    gn = gn_ref[...]
    hw = n_heads * LANE
    tr = x.shape[0]
    ones_col = (lax.broadcasted_iota(jnp.int32, (tr, LANE), 1) == 0).astype(v_ref.dtype)
    for h in range(n_heads):
        kn = acc[:, h * LANE:(h + 1) * LANE]
        ss = jnp.sum(kn * kn, axis=-1, keepdims=True) + ss_r
        inv = lax.rsqrt(ss * (1.0 / QK_DIM) + EPS)
        k_ref[h, :, 0:LANE] = (kn * inv * gn).astype(k_ref.dtype)
        k_ref[h, :, LANE:HEAD_PAD] = (rope * inv).astype(k_ref.dtype)
        v_ref[h, :, 0:LANE] = acc[:, hw + h * LANE: hw + (h + 1) * LANE].astype(v_ref.dtype)
        v_ref[h, :, LANE:HEAD_PAD] = ones_col


def _kvprep(zkv, a_norm, w_p, gn, gr, gs, cos, sin, n_heads, kv_lora):
    b, s, zw = zkv.shape
    tr = _tile(s, 256)
    nw = w_p.shape[1]
    out = jax.ShapeDtypeStruct((b, n_heads, s, HEAD_PAD), BF16)
    ospec = pl.BlockSpec((None, n_heads, tr, HEAD_PAD), lambda bi, i: (bi, 0, i, 0))
    return pl.pallas_call(
        functools.partial(_kvprep_body, n_heads=n_heads, kv_lora=kv_lora),
        grid=(b, s // tr),
        in_specs=[pl.BlockSpec((None, tr, zw), lambda bi, i: (bi, i, 0)),
                  pl.BlockSpec((1, kv_lora), lambda bi, i: (0, 0)),
                  pl.BlockSpec((kv_lora, nw), lambda bi, i: (0, 0)),
                  pl.BlockSpec((1, LANE), lambda bi, i: (0, 0)),
                  pl.BlockSpec((1, LANE), lambda bi, i: (0, 0)),
                  pl.BlockSpec((1, LANE), lambda bi, i: (0, 0)),
                  pl.BlockSpec((tr, LANE), lambda bi, i: (i, 0)),
                  pl.BlockSpec((tr, LANE), lambda bi, i: (i, 0))],
        out_specs=[ospec, ospec],
        out_shape=[out, out],
        compiler_params=_cparams(("parallel", "parallel")),
        name="kv_prep",
    )(zkv, a_norm, w_p, gn, gr, gs, cos, sin)


def _attn_body(*refs, n_src, chunks):
    q_ref = refs[0]
    kv = refs[1:1 + 2 * n_src]
    gate_ref = refs[1 + 2 * n_src]
    o_ref = refs[2 + 2 * n_src]
    m_ref, acc_ref = refs[3 + 2 * n_src], refs[4 + 2 * n_src]
    q = q_ref[...]
    m_ref[...] = jnp.full(m_ref.shape, -jnp.inf, F32)
    acc_ref[...] = jnp.zeros(acc_ref.shape, F32)
    for si in range(n_src):
        k_ref, v_ref = kv[2 * si], kv[2 * si + 1]
        tk, n_chunks = chunks[si]

        def step(c, carry, k_ref=k_ref, v_ref=v_ref, tk=tk):
            start = pl.multiple_of(c * tk, tk)
            k = k_ref[pl.ds(start, tk), :]
            v = v_ref[pl.ds(start, tk), :]
            s = lax.dot_general(q, k, (((1,), (1,)), ((), ())), preferred_element_type=F32)
            m_old = m_ref[...]
            m_new = jnp.maximum(m_old, jnp.max(s, axis=-1, keepdims=True))
            p = jnp.exp(s - m_new).astype(BF16)
            acc_ref[...] = acc_ref[...] * jnp.exp(m_old - m_new) + _dot(p, v)
            m_ref[...] = m_new
            return carry

        lax.fori_loop(0, n_chunks, step, 0)
    acc = acc_ref[...]
    out = acc[:, 0:V_DIM] / acc[:, V_DIM:V_DIM + 1]
    o_ref[...] = (out * gate_ref[...].astype(F32)).astype(o_ref.dtype)


def _attention(q, sources, gates, gate_col, n_heads):
    b, _, s, _ = q.shape
    tq = _tile(s, 512)
    in_specs = [pl.BlockSpec((None, None, tq, HEAD_PAD), lambda bi, h, i: (bi, h, i, 0))]
    args = [q]
    chunks = []
    for k, v in sources:
        sk = k.shape[2]
        tk = _tile(sk, 512)
        chunks.append((tk, sk // tk))
        spec = pl.BlockSpec((None, None, sk, HEAD_PAD), lambda bi, h, i: (bi, h, 0, 0))
        in_specs += [spec, spec]
        args += [k, v]
    gb = gate_col // V_DIM
    in_specs.append(pl.BlockSpec((None, tq, V_DIM), lambda bi, h, i: (bi, i, gb + h)))
    args.append(gates)
    return pl.pallas_call(
        functools.partial(_attn_body, n_src=len(sources), chunks=tuple(chunks)),
        grid=(b, n_heads, s // tq),
        in_specs=in_specs,
        out_specs=pl.BlockSpec((None, tq, V_DIM), lambda bi, h, i: (bi, i, h)),
        out_shape=jax.ShapeDtypeStruct((b, s, n_heads * V_DIM), BF16),
        scratch_shapes=[pltpu.VMEM((tq, 1), F32), pltpu.VMEM((tq, HEAD_PAD), F32)],
        compiler_params=_cparams(("parallel", "parallel", "arbitrary")),
        name="attention",
    )(*args)


def _fft_factors(s):
    n1 = 128 if s % 128 == 0 and s // 128 >= 16 else 16
    return n1, s // n1


@functools.lru_cache(maxsize=None)
def _fft_tables(s, c):
    n1, n2 = _fft_factors(s)
    k1 = np.arange(n1, dtype=np.float64)
    a1 = 2.0 * np.pi * np.outer(k1, k1) / n1
    t1 = np.concatenate([np.cos(a1), -np.sin(a1)], axis=0)
    at = 2.0 * np.pi * np.outer(np.arange(n2, dtype=np.float64), k1) / s
    twr = np.cos(at)[:, :, None]
    twi = (-np.sin(at))[:, :, None]
    k2 = np.arange(n2, dtype=np.float64)
    a3 = 2.0 * np.pi * np.outer(k2, k2) / n2
    c3, s3 = np.cos(a3), np.sin(a3)
    t3 = np.block([[c3, s3], [-s3, c3]])
    kc = np.arange(c, dtype=np.float64)
    ac = 2.0 * np.pi * np.outer(kc, kc) / c
    tc = np.stack([np.cos(ac), np.sin(ac)], axis=0) / math.sqrt(float(s) * float(c))

    def hl(t):
        t32 = t.astype(np.float32)
        hi = t32.astype(BF16)
        lo = (t32 - hi.astype(np.float32)).astype(BF16)
        return hi, lo

    return dict(t1=hl(t1), twr=twr.astype(np.float32), twi=twi.astype(np.float32), t3=hl(t3), tc=hl(tc))


def _fft1_body(u_ref, thi_ref, tlo_ref, twr_ref, twi_ref, y_ref, *, n1):
    u_hi, u_lo = _split_hi_lo(u_ref[...])
    y = _dot3(thi_ref[...], tlo_ref[...], u_hi, u_lo)
    yr, yi = y[0:n1], y[n1:2 * n1]
    tr, ti = twr_ref[...], twi_ref[...]
    y_ref[0] = yr * tr - yi * ti
    y_ref[1] = yr * ti + yi * tr


def _fft3_body(y_ref, thi_ref, tlo_ref, x_ref):
    y_hi, y_lo = _split_hi_lo(y_ref[...])
    x_ref[...] = _dot3(thi_ref[...], tlo_ref[...], y_hi, y_lo)


def _fftc_body(x_ref, thi_ref, tlo_ref, wf_ref, gate_ref, o_ref):
    xr_hi, xr_lo = _split_hi_lo(x_ref[0])
    xi_hi, xi_lo = _split_hi_lo(x_ref[1])
    z = _dot3(xr_hi, xr_lo, thi_ref[0], tlo_ref[0]) + _dot3(xi_hi, xi_lo, thi_ref[1], tlo_ref[1])
    y = _dot(z.astype(BF16), wf_ref[...])
    o_ref[...] = (y * gate_ref[...].astype(F32)).astype(o_ref.dtype)


def _fourier_branch(u, w_fnet, gates, gate_col):
    b, s, c = u.shape
    n1, n2 = _fft_factors(s)
    tabs = _fft_tables(s, c)
    bf = lambda t: jnp.asarray(t, dtype=BF16)
    t1h, t1l = map(bf, tabs["t1"])
    t3h, t3l = map(bf, tabs["t3"])
    tch, tcl = map(bf, tabs["tc"])
    twr, twi = jnp.asarray(tabs["twr"]), jnp.asarray(tabs["twi"])

    const2 = lambda bi, j: (0, 0)
    y = pl.pallas_call(
        functools.partial(_fft1_body, n1=n1),
        grid=(b, n2),
        in_specs=[pl.BlockSpec((None, n1, c), lambda bi, j: (bi, 0, j)),
                  pl.BlockSpec((2 * n1, n1), const2),
                  pl.BlockSpec((2 * n1, n1), const2),
                  pl.BlockSpec((None, n1, 1), lambda bi, j: (j, 0, 0)),
                  pl.BlockSpec((None, n1, 1), lambda bi, j: (j, 0, 0))],
        out_specs=pl.BlockSpec((None, 2, None, n1, c), lambda bi, j: (bi, 0, j, 0, 0)),
        out_shape=jax.ShapeDtypeStruct((b, 2, n2, n1, c), F32),
        compiler_params=_cparams(("parallel", "parallel")),
        name="fft_stage1",
    )(u.reshape(b, n1, n2 * c), t1h, t1l, twr, twi)

    cols = n1 * c
    tn = _tile(cols, 4096, LANE)
    x = pl.pallas_call(
        _fft3_body,
        grid=(b, cols // tn),
        in_specs=[pl.BlockSpec((None, 2 * n2, tn), lambda bi, j: (bi, 0, j)),
                  pl.BlockSpec((2 * n2, 2 * n2), const2),
                  pl.BlockSpec((2 * n2, 2 * n2), const2)],
        out_specs=pl.BlockSpec((None, 2 * n2, tn), lambda bi, j: (bi, 0, j)),
        out_shape=jax.ShapeDtypeStruct((b, 2 * n2, cols), F32),
        compiler_params=_cparams(("parallel", "parallel")),
        name="fft_stage2",
    )(y.reshape(b, 2 * n2, cols), t3h, t3l)

    tm = _tile(s, 512)
    gb = gate_col // c
    const3 = lambda bi, i: (0, 0, 0)
    return pl.pallas_call(
        _fftc_body,
        grid=(b, s // tm),
        in_specs=[pl.BlockSpec((None, 2, tm, c), lambda bi, i: (bi, 0, i, 0)),
                  pl.BlockSpec((2, c, c), const3),
                  pl.BlockSpec((2, c, c), const3),
                  pl.BlockSpec((c, c), lambda bi, i: (0, 0)),
                  pl.BlockSpec((None, tm, c), lambda bi, i: (bi, i, gb))],
        out_specs=pl.BlockSpec((None, tm, c), lambda bi, i: (bi, i, 0)),
        out_shape=jax.ShapeDtypeStruct((b, s, c), BF16),
        compiler_params=_cparams(("parallel", "parallel")),
        name="fft_channel",
    )(x.reshape(b, 2, s, c), tch, tcl, w_fnet, gates)


def _conv_body(zl_ref, zc_ref, zr_ref, cw_ref, cb_ref, lg_ref, lb_ref, w_ref, gate_ref, o_ref, ext_ref, *,
               c, ts, n_tiles):
    i = pl.program_id(1)

    def glu(z):
        return z[:, 0:c] * jax.nn.sigmoid(z[:, c:2 * c])

    left = jnp.where(i > 0, glu(zl_ref[...]), 0.0)
    right = jnp.where(i < n_tiles - 1, glu(zr_ref[...]), 0.0)
    ext_ref[0:CONV_HALO, :] = left
    ext_ref[CONV_HALO:CONV_HALO + ts, :] = glu(zc_ref[...])
    ext_ref[CONV_HALO + ts:2 * CONV_HALO + ts, :] = right
    off = CONV_HALO - CONV_W // 2
    acc = jnp.zeros((ts, c), F32) + cb_ref[...]
    for j in range(CONV_W):
        acc = acc + ext_ref[off + j:off + j + ts, :] * cw_ref[j:j + 1, :]
    mu = jnp.mean(acc, axis=-1, keepdims=True)
    d = acc - mu
    var = jnp.mean(d * d, axis=-1, keepdims=True)
    y = _silu(d * lax.rsqrt(var + EPS) * lg_ref[...] + lb_ref[...])
    out = _dot(y.astype(BF16), w_ref[...])
    o_ref[...] = (out * gate_ref[...].astype(F32)).astype(o_ref.dtype)


def _conv_branch(zglu, conv_w, conv_b, cln_g, cln_b, w_pw2, gates, gate_col):
    b, s, c2 = zglu.shape
    c = c2 // 2
    ts = _tile(s, 256, CONV_HALO)
    n_tiles = s // ts
    hb = ts // CONV_HALO
    n_hblk = s // CONV_HALO
    gb = gate_col // c
    row = lambda bi, i: (0, 0)
    return pl.pallas_call(
        functools.partial(_conv_body, c=c, ts=ts, n_tiles=n_tiles),
        grid=(b, n_tiles),
        in_specs=[pl.BlockSpec((None, CONV_HALO, c2), lambda bi, i: (bi, jnp.maximum(i * hb - 1, 0), 0)),
                  pl.BlockSpec((None, ts, c2), lambda bi, i: (bi, i, 0)),
                  pl.BlockSpec((None, CONV_HALO, c2),
                               lambda bi, i: (bi, jnp.minimum((i + 1) * hb, n_hblk - 1), 0)),
                  pl.BlockSpec((CONV_W, c), row),
                  pl.BlockSpec((1, c), row),
                  pl.BlockSpec((1, c), row),
                  pl.BlockSpec((1, c), row),
                  pl.BlockSpec((c, c), row),
                  pl.BlockSpec((None, ts, c), lambda bi, i: (bi, i, gb))],
        out_specs=pl.BlockSpec((None, ts, c), lambda bi, i: (bi, i, 0)),
        out_shape=jax.ShapeDtypeStruct((b, s, c), BF16),
        scratch_shapes=[pltpu.VMEM((ts + 2 * CONV_HALO, c), F32)],
        compiler_params=_cparams(("parallel", "arbitrary")),
        name="conv_branch",
    )(zglu, zglu, zglu, conv_w, conv_b, cln_g, cln_b, w_pw2, gates)


def _merge_body(yf_ref, ym_ref, yc_ref, wf_ref, wm_ref, wc_ref, gf_ref, gm_ref, gc_ref, o_ref):
    acc = gf_ref[...].astype(F32) * _dot(yf_ref[...], wf_ref[...])
    acc = acc + gm_ref[...].astype(F32) * _dot(ym_ref[...], wm_ref[...])
    acc = acc + gc_ref[...].astype(F32) * _dot(yc_ref[...], wc_ref[...])
    o_ref[...] = acc.astype(o_ref.dtype)


def _merge(yf, ym, yc, wf, wm, wc, g):
    m = yf.shape[0]
    d = wf.shape[1]
    tm = _tile(m, 512)
    tn = _tile(d, 1024, LANE)
    nj = d // tn
    yspec = lambda arr: pl.BlockSpec((tm, arr.shape[1]), lambda j, i: (i, 0))
    wspec = lambda arr: pl.BlockSpec((arr.shape[0], tn), lambda j, i: (0, j))
    gspec = lambda t: pl.BlockSpec((tm, tn), lambda j, i, t=t: (i, t * nj + j))
    return pl.pallas_call(
        _merge_body,
        grid=(nj, m // tm),
        in_specs=[yspec(yf), yspec(ym), yspec(yc), wspec(wf), wspec(wm), wspec(wc),
                  gspec(0), gspec(1), gspec(2)],
        out_specs=pl.BlockSpec((tm, tn), lambda j, i: (i, j)),
        out_shape=jax.ShapeDtypeStruct((m, d), BF16),
        compiler_params=_cparams(("parallel", "parallel")),
        name="merge",
    )(yf, ym, yc, wf, wm, wc, g, g, g)


def _rope_tables(n_tok):
    half = QK_ROPE // 4
    inv = ROPE_BASE ** (-jnp.arange(half, dtype=F32) / half)
    t = jnp.arange(n_tok, dtype=jnp.int32)
    rows = (t // GRID_W).astype(F32)[:, None] * inv[None, :]
    cols = (t % GRID_W).astype(F32)[:, None] * inv[None, :]
    cr, sr, cc, sc = jnp.cos(rows), jnp.sin(rows), jnp.cos(cols), jnp.sin(cols)
    pad1 = jnp.ones((n_tok, LANE - QK_ROPE), F32)
    pad0 = jnp.zeros((n_tok, LANE - QK_ROPE), F32)
    cos = jnp.concatenate([cr, cr, cc, cc, pad1], axis=1)
    sin = jnp.concatenate([-sr, sr, -sc, sc, pad0], axis=1)
    return cos, sin


def _swap_perm():
    q = QK_ROPE // 4
    return np.concatenate([np.arange(q, 2 * q), np.arange(0, q), np.arange(3 * q, 4 * q), np.arange(2 * q, 3 * q)])


def _pad_lanes(w, width=LANE):
    return jnp.pad(w, [(0, 0)] * (w.ndim - 1) + [(0, width - w.shape[-1])])


def _prep_layer(l, dims, w_in, w_uq, w_ukv, q_norm, k_norm, w_fnet, w_pw2, w_br_f, w_br_m, w_br_c, w_out):
    d, f, ql, kvl, h, c = dims
    mla = h * V_DIM
    off_fg = f
    off_q = 2 * f
    off_kv = off_q + ql
    off_kr = off_kv + kvl
    off_mg = off_kr + QK_ROPE
    off_glu = off_mg + mla
    off_cg = off_glu + 2 * c
    off_merge = off_cg + c
    perm = _swap_perm()
    wi = w_in[l]
    w_u = wi[:, 0:f].astype(BF16)
    w_gate = jnp.concatenate([wi[:, off_fg:off_q], wi[:, off_mg:off_glu], wi[:, off_cg:off_merge]],
                             axis=1).astype(BF16)
    w_merge = wi[:, off_merge:].astype(BF16)
    w_glu = wi[:, off_glu:off_cg].astype(BF16)
    w_q = wi[:, off_q:off_kv].astype(BF16)
    kr = wi[:, off_kr:off_mg]
    w_kv = jnp.concatenate([wi[:, off_kv:off_kr], _pad_lanes(kr), _pad_lanes(kr[:, perm])], axis=1).astype(BF16)

    wq3 = w_uq[l].reshape(ql, h, QK_DIM)
    q_nope = wq3[:, :, :QK_NOPE].reshape(ql, h * LANE)
    q_rope = wq3[:, :, QK_NOPE:]
    w_uq_p = jnp.concatenate([q_nope, _pad_lanes(q_rope).reshape(ql, h * LANE),
                              _pad_lanes(q_rope[:, :, perm]).reshape(ql, h * LANE)], axis=1).astype(BF16)
    wkv3 = w_ukv[l].reshape(kvl, h, QK_NOPE + V_DIM)
    w_ukv_p = jnp.concatenate([wkv3[:, :, :QK_NOPE].reshape(kvl, h * LANE),
                               wkv3[:, :, QK_NOPE:].reshape(kvl, h * LANE)], axis=1).astype(BF16)

    def gains(g):
        return (g[None, :QK_NOPE], _pad_lanes(g[None, QK_NOPE:]), _pad_lanes(g[None, QK_NOPE:][:, perm]))

    return dict(w_u=w_u, w_gate=w_gate, w_merge=w_merge, w_glu=w_glu, w_q=w_q, w_kv=w_kv,
                w_uq=w_uq_p, w_ukv=w_ukv_p, qg=gains(q_norm[l]), kg=gains(k_norm[l]),
                w_fnet=w_fnet[l].astype(BF16), w_pw2=w_pw2[l].astype(BF16),
                w_br_f=w_br_f[l].astype(BF16), w_br_m=w_br_m[l].astype(BF16), w_br_c=w_br_c[l].astype(BF16),
                w_out=w_out[l].astype(BF16))


def _kv_stream(h2, bsz, s, w, kv_a_norm, cos, sin, n_heads, kv_lora):
    zkv = _mm(h2, w["w_kv"], out_dtype=F32, tn=w["w_kv"].shape[1], name="in_proj_kv")
    gn, gr, gs = w["kg"]
    return _kvprep(zkv.reshape(bsz, s, -1), kv_a_norm, w["w_ukv"], gn, gr, gs, cos, sin, n_heads, kv_lora)


def _full_stream(x, h, kv_own, kv_ctx, gate, w, p, cos, sin, dims):
    d, f, ql, kvl, n_heads, c = dims
    bsz, s, _ = x.shape
    m = bsz * s
    h2 = h.reshape(m, d)
    zu = _mm(h2, w["w_u"], out_dtype=F32, name="in_proj_u")
    gates = _mm(h2, w["w_gate"], out_dtype=BF16, epi=_silu, name="in_proj_gates")
    gmerge = _mm(h2, w["w_merge"], out_dtype=BF16, epi=jax.nn.sigmoid, name="in_proj_merge")
    zglu = _mm(h2, w["w_glu"], out_dtype=F32, name="in_proj_glu")
    zq = _mm(h2, w["w_q"], out_dtype=F32, tn=ql, name="in_proj_q")

    gn, gr, gs = w["qg"]
    q = _qprep(zq.reshape(bsz, s, ql), p["q_a_norm"], w["w_uq"], gn, gr, gs, cos, sin, n_heads)
    sources = ([kv_ctx] if kv_ctx is not None else []) + [kv_own]
    gates3 = gates.reshape(bsz, s, -1)
    y_m = _attention(q, sources, gates3, f, n_heads)
    y_f = _fourier_branch(zu.reshape(bsz, s, f), w["w_fnet"], gates3, 0)
    y_c = _conv_branch(zglu.reshape(bsz, s, 2 * c), p["conv_w"], p["conv_b"], p["cln_g"], p["cln_b"],
                       w["w_pw2"], gates3, f + n_heads * V_DIM)
    merged = _merge(y_f.reshape(m, f), y_m.reshape(m, -1), y_c.reshape(m, c),
                    w["w_br_f"], w["w_br_m"], w["w_br_c"], gmerge)
    out = _mm(merged, w["w_out"], out_dtype=F32, rows_per_batch=s,
              epi=lambda acc, xv, gv: xv + gv * acc,
              extras=(("mn", x.reshape(m, d)), ("bn", gate)), name="out_proj")
    return out.reshape(bsz, s, d)


def kernel(x, c, ctx, c_ctx, norm_g, w_ada, b_ada, w_in, q_a_norm, w_uq, kv_a_norm, w_ukv, q_norm, k_norm,
           w_fnet, conv_w, conv_b, cln_g, cln_b, w_pw2, w_br_f, w_br_m, w_br_c, w_out):
    bsz, n_tok, d = x.shape
    n_ctx = ctx.shape[1]
    depth = w_in.shape[0]
    f = w_fnet.shape[1]
    ql = q_a_norm.shape[1]
    kvl = kv_a_norm.shape[1]
    n_heads = w_uq.shape[2] // QK_DIM
    cdim = conv_b.shape[1]
    dims = (d, f, ql, kvl, n_heads, cdim)

    cos_l, sin_l = _rope_tables(n_tok)
    cos_c = jnp.ones((n_ctx, LANE), F32)
    sin_c = jnp.zeros((n_ctx, LANE), F32)

    n_rows = -(-(bsz + 1) // 8) * 8
    cond = jnp.concatenate([c, c_ctx[None, :], jnp.zeros((n_rows - bsz - 1, d), F32)], axis=0)

    xl, xc = x, ctx
    for l in range(depth):
        last = l == depth - 1
        w = _prep_layer(l, dims, w_in, w_uq, w_ukv, q_norm, k_norm, w_fnet, w_pw2, w_br_f, w_br_m, w_br_c, w_out)
        p = dict(q_a_norm=q_a_norm[l][None], conv_w=conv_w[l], conv_b=conv_b[l][None],
                 cln_g=cln_g[l][None], cln_b=cln_b[l][None])
        mod = _mm(cond, w_ada[l], out_dtype=F32, tm=n_rows, tn=1024, tk=2048, a_act=_silu,
                  epi=lambda acc, bias: acc + bias, extras=(("n", b_ada[l][None]),), name="adaln")
        shift_l, scale_l, gate_l = (mod[:bsz, i * d:(i + 1) * d][:, None, :] for i in range(3))
        shift_c, scale_c, gate_c = (jnp.broadcast_to(mod[bsz, i * d:(i + 1) * d][None, None, :], (bsz, 1, d))
                                    for i in range(3))
        g = norm_g[l][None]
        hl = _norm_mod(xl, g, scale_l, shift_l)
        hc = _norm_mod(xc, g, scale_c, shift_c)
        kv_c = _kv_stream(hc.reshape(bsz * n_ctx, d), bsz, n_ctx, w, kv_a_norm[l][None], cos_c, sin_c, n_heads, kvl)
        kv_l = _kv_stream(hl.reshape(bsz * n_tok, d), bsz, n_tok, w, kv_a_norm[l][None], cos_l, sin_l, n_heads, kvl)
        new_xl = _full_stream(xl, hl, kv_l, kv_c, gate_l, w, p, cos_l, sin_l, dims)
        if not last:
            xc = _full_stream(xc, hc, kv_c, None, gate_c, w, p, cos_c, sin_c, dims)
        xl = new_xl
    return xl
```

```python
import functools
import math

import numpy as np
import jax
import jax.numpy as jnp
from jax import lax
from jax.experimental import pallas as pl
from jax.experimental.pallas import tpu as pltpu

F32 = jnp.float32
BF16 = jnp.bfloat16

EPS = 1e-6
QK_NOPE = 128
QK_ROPE = 64
V_DIM = 128
QK_DIM = QK_NOPE + QK_ROPE
HEAD_PAD = 256
LANE = 128
GRID_W = 64
CONV_W = 31
CONV_HALO = 16
ROPE_BASE = 10000.0
N_BRANCH = 3
ATTN_TQ = 512
ATTN_TK = 1024
VMEM_LIMIT = 52 * 1024 * 1024


def _cparams(sem):
    return pltpu.CompilerParams(dimension_semantics=sem, vmem_limit_bytes=VMEM_LIMIT)


def _tile(n, pref, align=8):
    if n <= pref:
        return n
    t = (pref // align) * align
    while t >= align:
        if n % t == 0:
            return t
        t -= align
    return n


def _silu(v):
    return v * jax.nn.sigmoid(v)


def _split_hi_lo(v):
    hi = v.astype(BF16)
    lo = (v - hi.astype(F32)).astype(BF16)
    return hi, lo


def _dot(a, b):
    return jnp.dot(a, b, preferred_element_type=F32)


def _dot3(a_hi, a_lo, b_hi, b_lo):
    return _dot(a_hi, b_hi) + (_dot(a_hi, b_lo) + _dot(a_lo, b_hi))


def _mm_body(*refs, nk, n_extra, epi, a_act):
    a_ref, b_ref = refs[0], refs[1]
    extra = refs[2:2 + n_extra]
    o_ref = refs[2 + n_extra]
    a = a_ref[...]
    if a_act is not None:
        a = a_act(a.astype(F32))
    p = _dot(a.astype(BF16), b_ref[...].astype(BF16))
    if nk == 1:
        o_ref[...] = epi(p, *[e[...] for e in extra]).astype(o_ref.dtype)
    else:
        acc_ref = refs[3 + n_extra]
        k = pl.program_id(2)

        @pl.when(k == 0)
        def _():
            acc_ref[...] = p

        @pl.when(k > 0)
        def _():
            acc_ref[...] += p

        @pl.when(k == nk - 1)
        def _():
            o_ref[...] = epi(acc_ref[...], *[e[...] for e in extra]).astype(o_ref.dtype)


def _mm(a, b, *, out_dtype, tm=1024, tn=512, tk=4096, epi=None, extras=(), a_act=None,
        rows_per_batch=None, b_cols=None, name="mm"):
    m, kdim = a.shape
    col0, n = (0, b.shape[1]) if b_cols is None else b_cols
    tm = _tile(m, tm)
    if rows_per_batch is not None:
        tm = _tile(rows_per_batch, tm)
    tn = _tile(n, tn, LANE)
    while col0 % tn:
        tn = _tile(n, tn - LANE, LANE)
    tk = _tile(kdim, tk, LANE)
    nk = kdim // tk
    jb = col0 // tn
    if epi is None:
        epi = lambda p: p
    in_specs = [pl.BlockSpec((tm, tk), lambda i, j, k: (i, k)),
                pl.BlockSpec((tk, tn), lambda i, j, k: (k, j + jb))]
    args = [a, b]
    for kind, arr in extras:
        if kind == "mn":
            in_specs.append(pl.BlockSpec((tm, tn), lambda i, j, k: (i, j)))
        elif kind == "n":
            in_specs.append(pl.BlockSpec((1, tn), lambda i, j, k: (0, j)))
        else:
            bpt = rows_per_batch // tm
            in_specs.append(pl.BlockSpec((None, 1, tn), lambda i, j, k, bpt=bpt: (i // bpt, 0, j)))
        args.append(arr)
    scratch = [pltpu.VMEM((tm, tn), F32)] if nk > 1 else []
    return pl.pallas_call(
        functools.partial(_mm_body, nk=nk, n_extra=len(extras), epi=epi, a_act=a_act),
        grid=(m // tm, n // tn, nk),
        in_specs=in_specs,
        out_specs=pl.BlockSpec((tm, tn), lambda i, j, k: (i, j)),
        out_shape=jax.ShapeDtypeStruct((m, n), out_dtype),
        scratch_shapes=scratch,
        compiler_params=_cparams(("parallel", "parallel", "arbitrary")),
        name=name,
    )(*args)


def _shift_cast_body(main_ref, next_ref, o_ref):
    x = jnp.concatenate([main_ref[...], next_ref[...]], axis=1)
    width = x.shape[1]
    o_ref[...] = pltpu.roll(x, width - LANE // 2, axis=1)[:, 0:width - LANE].astype(o_ref.dtype)


def _shift_cast(w, l, start, width):
    kdim, n_all = w.shape[1], w.shape[2]
    base = start - LANE // 2
    tn = 2 * LANE if base % (2 * LANE) == 0 and width % (2 * LANE) == 0 else LANE
    assert base % tn == 0 and width % tn == 0
    tk = _tile(kdim, 1024)
    jb = base // tn
    r = tn // LANE
    last = (n_all - 1) // LANE
    return pl.pallas_call(
        _shift_cast_body,
        grid=(kdim // tk, width // tn),
        in_specs=[pl.BlockSpec((None, tk, tn), lambda i, j: (l, i, jb + j)),
                  pl.BlockSpec((None, tk, LANE), lambda i, j: (l, i, jnp.minimum((jb + j + 1) * r, last)))],
        out_specs=pl.BlockSpec((tk, tn), lambda i, j: (i, j)),
        out_shape=jax.ShapeDtypeStruct((kdim, width), BF16),
        compiler_params=_cparams(("parallel", "parallel")),
        name="shift_cast",
    )(w, w)


def _norm_body(x_ref, g_ref, scale_ref, shift_ref, o_ref):
    x = x_ref[...]
    r = lax.rsqrt(jnp.mean(x * x, axis=-1, keepdims=True) + EPS)
    y = x * r * g_ref[...]
    o_ref[...] = (y * (1.0 + scale_ref[...]) + shift_ref[...]).astype(o_ref.dtype)


def _norm_mod(x, g, scale, shift):
    b, s, d = x.shape
    tr = _tile(s, 256)
    return pl.pallas_call(
        _norm_body,
        grid=(b, s // tr),
        in_specs=[pl.BlockSpec((None, tr, d), lambda bi, i: (bi, i, 0)),
                  pl.BlockSpec((1, d), lambda bi, i: (0, 0)),
                  pl.BlockSpec((None, 1, d), lambda bi, i: (bi, 0, 0)),
                  pl.BlockSpec((None, 1, d), lambda bi, i: (bi, 0, 0))],
        out_specs=pl.BlockSpec((None, tr, d), lambda bi, i: (bi, i, 0)),
        out_shape=jax.ShapeDtypeStruct((b, s, d), BF16),
        compiler_params=_cparams(("parallel", "parallel")),
        name="norm_mod",
    )(x, g, scale, shift)


def _qprep_body(z_ref, an_ref, w_ref, gn_ref, gr_ref, gs_ref, cos_ref, sin_ref, q_ref, *, n_heads, scale):
    x = z_ref[...]
    r = lax.rsqrt(jnp.mean(x * x, axis=-1, keepdims=True) + EPS)
    xn = (x * r * an_ref[...]).astype(BF16)
    acc = _dot(xn, w_ref[...])
    cos = cos_ref[...] * gr_ref[...]
    sin = sin_ref[...] * gs_ref[...]
    gn = gn_ref[...]
    hw = n_heads * LANE
    for h in range(n_heads):
        nope = acc[:, h * LANE:(h + 1) * LANE]
        rp = acc[:, hw + h * LANE: hw + (h + 1) * LANE]
        rs = acc[:, 2 * hw + h * LANE: 2 * hw + (h + 1) * LANE]
        ss = jnp.sum(nope * nope, axis=-1, keepdims=True) + jnp.sum(rp * rp, axis=-1, keepdims=True)
        inv = lax.rsqrt(ss * (1.0 / QK_DIM) + EPS) * scale
        q_ref[h, :, 0:LANE] = (nope * inv * gn).astype(q_ref.dtype)
        q_ref[h, :, LANE:HEAD_PAD] = ((rp * cos + rs * sin) * inv).astype(q_ref.dtype)


def _qprep(zq, a_norm, w_p, gn, gr, gs, cos, sin, n_heads):
    b, s, ql = zq.shape
    tr = _tile(s, 256)
    nw = w_p.shape[1]
    return pl.pallas_call(
        functools.partial(_qprep_body, n_heads=n_heads, scale=QK_DIM ** -0.5),
        grid=(b, s // tr),
        in_specs=[pl.BlockSpec((None, tr, ql), lambda bi, i: (bi, i, 0)),
                  pl.BlockSpec((1, ql), lambda bi, i: (0, 0)),
                  pl.BlockSpec((ql, nw), lambda bi, i: (0, 0)),
                  pl.BlockSpec((1, LANE), lambda bi, i: (0, 0)),
                  pl.BlockSpec((1, LANE), lambda bi, i: (0, 0)),
                  pl.BlockSpec((1, LANE), lambda bi, i: (0, 0)),
                  pl.BlockSpec((tr, LANE), lambda bi, i: (i, 0)),
                  pl.BlockSpec((tr, LANE), lambda bi, i: (i, 0))],
        out_specs=pl.BlockSpec((None, n_heads, tr, HEAD_PAD), lambda bi, i: (bi, 0, i, 0)),
        out_shape=jax.ShapeDtypeStruct((b, n_heads, s, HEAD_PAD), BF16),
        compiler_params=_cparams(("parallel", "parallel")),
        name="q_prep",
    )(zq, a_norm, w_p, gn, gr, gs, cos, sin)


def _kvprep_body(z_ref, an_ref, w_ref, gn_ref, gr_ref, gs_ref, cos_ref, sin_ref, k_ref, v_ref, *,
                 n_heads, kv_lora):
    x = z_ref[:, 0:kv_lora]
    kr = z_ref[:, kv_lora:kv_lora + LANE]
    krs = z_ref[:, kv_lora + LANE:kv_lora + 2 * LANE]
    r = lax.rsqrt(jnp.mean(x * x, axis=-1, keepdims=True) + EPS)
    xn = (x * r * an_ref[...]).astype(BF16)
    acc = _dot(xn, w_ref[...])
    ss_r = jnp.sum(kr * kr, axis=-1, keepdims=True)
    rope = kr * (cos_ref[...] * gr_ref[...]) + krs * (sin_ref[...] * gs_ref[...])
    gn = gn_ref[...]
    hw = n_heads * LANE
    tr = x.shape[0]
    ones_col = (lax.broadcasted_iota(jnp.int32, (tr, LANE), 1) == 0).astype(v_ref.dtype)
    for h in range(n_heads):
        kn = acc[:, h * LANE:(h + 1) * LANE]
        ss = jnp.sum(kn * kn, axis=-1, keepdims=True) + ss_r
        inv = lax.rsqrt(ss * (1.0 / QK_DIM) + EPS)
        k_ref[h, :, 0:LANE] = (kn * inv * gn).astype(k_ref.dtype)
        k_ref[h, :, LANE:HEAD_PAD] = (rope * inv).astype(k_ref.dtype)
        v_ref[h, :, 0:LANE] = acc[:, hw + h * LANE: hw + (h + 1) * LANE].astype(v_ref.dtype)
        v_ref[h, :, LANE:HEAD_PAD] = ones_col


def _kvprep(zkv, a_norm, w_p, gn, gr, gs, cos, sin, n_heads, kv_lora):
    b, s, zw = zkv.shape
    tr = _tile(s, 256)
    nw = w_p.shape[1]
    out = jax.ShapeDtypeStruct((b, n_heads, s, HEAD_PAD), BF16)
    ospec = pl.BlockSpec((None, n_heads, tr, HEAD_PAD), lambda bi, i: (bi, 0, i, 0))
    return pl.pallas_call(
        functools.partial(_kvprep_body, n_heads=n_heads, kv_lora=kv_lora),
        grid=(b, s // tr),
        in_specs=[pl.BlockSpec((None, tr, zw), lambda bi, i: (bi, i, 0)),
                  pl.BlockSpec((1, kv_lora), lambda bi, i: (0, 0)),
                  pl.BlockSpec((kv_lora, nw), lambda bi, i: (0, 0)),
                  pl.BlockSpec((1, LANE), lambda bi, i: (0, 0)),
                  pl.BlockSpec((1, LANE), lambda bi, i: (0, 0)),
                  pl.BlockSpec((1, LANE), lambda bi, i: (0, 0)),
                  pl.BlockSpec((tr, LANE), lambda bi, i: (i, 0)),
                  pl.BlockSpec((tr, LANE), lambda bi, i: (i, 0))],
        out_specs=[ospec, ospec],
        out_shape=[out, out],
        compiler_params=_cparams(("parallel", "parallel")),
        name="kv_prep",
    )(zkv, a_norm, w_p, gn, gr, gs, cos, sin)


def _attn_body(*refs, n_src, chunks):
    q_ref = refs[0]
    kv = refs[1:1 + 2 * n_src]
    gate_ref = refs[1 + 2 * n_src]
    o_ref = refs[2 + 2 * n_src]
    q = q_ref[...]
    tq = q.shape[0]
    plan = []
    for si in range(n_src):
        tk, n_chunks = chunks[si]
        plan += [(kv[2 * si], kv[2 * si + 1], ci * tk, tk) for ci in range(n_chunks)]

    def scores(i):
        k_ref, _, start, tk = plan[i]
        return lax.dot_general(q, k_ref[start:start + tk, :], (((1,), (1,)), ((), ())),
                               preferred_element_type=F32)

    m = jnp.full((tq, LANE), -jnp.inf, F32)
    acc = jnp.zeros((tq, HEAD_PAD), F32)
    s_next = scores(0)
    for i in range(len(plan)):
        s = s_next
        if i + 1 < len(plan):
            s_next = scores(i + 1)
        _, v_ref, start, tk = plan[i]
        m_new = jnp.maximum(m, jnp.max(s, axis=-1, keepdims=True))
        p = jnp.exp(s - jnp.tile(m_new, (1, tk // LANE))).astype(BF16)
        alpha = jnp.exp(m - m_new)
        acc = acc * jnp.tile(alpha, (1, HEAD_PAD // LANE)) + _dot(p, v_ref[start:start + tk, :])
        m = m_new
    out = acc[:, 0:V_DIM] / acc[:, V_DIM:V_DIM + 1]
    o_ref[...] = (out * gate_ref[...].astype(F32)).astype(o_ref.dtype)


def _attention(q, sources, gates, gate_col, n_heads):
    b, _, s, _ = q.shape
    tq = _tile(s, ATTN_TQ)
    in_specs = [pl.BlockSpec((None, None, tq, HEAD_PAD), lambda bi, h, i: (bi, h, i, 0))]
    args = [q]
    chunks = []
    for k, v in sources:
        sk = k.shape[2]
        tk = _tile(sk, ATTN_TK)
        chunks.append((tk, sk // tk))
        spec = pl.BlockSpec((None, None, sk, HEAD_PAD), lambda bi, h, i: (bi, h, 0, 0))
        in_specs += [spec, spec]
        args += [k, v]
    gb = gate_col // V_DIM
    in_specs.append(pl.BlockSpec((None, tq, V_DIM), lambda bi, h, i: (bi, i, gb + h)))
    args.append(gates)
    return pl.pallas_call(
        functools.partial(_attn_body, n_src=len(sources), chunks=tuple(chunks)),
        grid=(b, n_heads, s // tq),
        in_specs=in_specs,
        out_specs=pl.BlockSpec((None, tq, V_DIM), lambda bi, h, i: (bi, i, h)),
        out_shape=jax.ShapeDtypeStruct((b, s, n_heads * V_DIM), BF16),
        compiler_params=_cparams(("parallel", "parallel", "arbitrary")),
        name="attention",
    )(*args)


def _fft_factors(s):
    n1 = 128 if s % 128 == 0 and s // 128 >= 16 else 16
    return n1, s // n1


@functools.lru_cache(maxsize=None)
def _fft_tables(s, c):
    n1, n2 = _fft_factors(s)
    k1 = np.arange(n1, dtype=np.float64)
    a1 = 2.0 * np.pi * np.outer(k1, k1) / n1
    t1 = np.concatenate([np.cos(a1), -np.sin(a1)], axis=0)
    at = 2.0 * np.pi * np.outer(np.arange(n2, dtype=np.float64), k1) / s
    twr = np.cos(at)[:, :, None]
    twi = (-np.sin(at))[:, :, None]
    k2 = np.arange(n2, dtype=np.float64)
    a3 = 2.0 * np.pi * np.outer(k2, k2) / n2
    c3, s3 = np.cos(a3), np.sin(a3)
    t3 = np.block([[c3, s3], [-s3, c3]])
    kc = np.arange(c, dtype=np.float64)
    ac = 2.0 * np.pi * np.outer(kc, kc) / c
    tc = np.stack([np.cos(ac), np.sin(ac)], axis=0) / math.sqrt(float(s) * float(c))

    def hl(t):
        t32 = t.astype(np.float32)
        hi = t32.astype(BF16)
        lo = (t32 - hi.astype(np.float32)).astype(BF16)
        return hi, lo

    return dict(t1=hl(t1), twr=twr.astype(np.float32), twi=twi.astype(np.float32), t3=hl(t3), tc=hl(tc))


def _fft1_body(u_ref, thi_ref, tlo_ref, twr_ref, twi_ref, y_ref, *, n1):
    u_hi, u_lo = _split_hi_lo(u_ref[...])
    y = _dot3(thi_ref[...], tlo_ref[...], u_hi, u_lo)
    yr, yi = y[0:n1], y[n1:2 * n1]
    tr, ti = twr_ref[...], twi_ref[...]
    y_ref[0] = yr * tr - yi * ti
    y_ref[1] = yr * ti + yi * tr


def _fft3_body(y_ref, thi_ref, tlo_ref, x_ref):
    y_hi, y_lo = _split_hi_lo(y_ref[...])
    x_ref[...] = _dot3(thi_ref[...], tlo_ref[...], y_hi, y_lo)


def _fftc_body(x_ref, thi_ref, tlo_ref, wf_ref, gate_ref, o_ref):
    xr_hi, xr_lo = _split_hi_lo(x_ref[0])
    xi_hi, xi_lo = _split_hi_lo(x_ref[1])
    z = _dot3(xr_hi, xr_lo, thi_ref[0], tlo_ref[0]) + _dot3(xi_hi, xi_lo, thi_ref[1], tlo_ref[1])
    y = _dot(z.astype(BF16), wf_ref[...])
    o_ref[...] = (y * gate_ref[...].astype(F32)).astype(o_ref.dtype)


def _fourier_branch(u, w_fnet, gates, gate_col):
    b, s, c = u.shape
    n1, n2 = _fft_factors(s)
    tabs = _fft_tables(s, c)
    bf = lambda t: jnp.asarray(t, dtype=BF16)
    t1h, t1l = map(bf, tabs["t1"])
    t3h, t3l = map(bf, tabs["t3"])
    tch, tcl = map(bf, tabs["tc"])
    twr, twi = jnp.asarray(tabs["twr"]), jnp.asarray(tabs["twi"])

    const2 = lambda bi, j: (0, 0)
    y = pl.pallas_call(
        functools.partial(_fft1_body, n1=n1),
        grid=(b, n2),
        in_specs=[pl.BlockSpec((None, n1, c), lambda bi, j: (bi, 0, j)),
                  pl.BlockSpec((2 * n1, n1), const2),
                  pl.BlockSpec((2 * n1, n1), const2),
                  pl.BlockSpec((None, n1, 1), lambda bi, j: (j, 0, 0)),
                  pl.BlockSpec((None, n1, 1), lambda bi, j: (j, 0, 0))],
        out_specs=pl.BlockSpec((None, 2, None, n1, c), lambda bi, j: (bi, 0, j, 0, 0)),
        out_shape=jax.ShapeDtypeStruct((b, 2, n2, n1, c), F32),
        compiler_params=_cparams(("parallel", "parallel")),
        name="fft_stage1",
    )(u.reshape(b, n1, n2 * c), t1h, t1l, twr, twi)

    cols = n1 * c
    tn = _tile(cols, 4096, LANE)
    x = pl.pallas_call(
        _fft3_body,
        grid=(b, cols // tn),
        in_specs=[pl.BlockSpec((None, 2 * n2, tn), lambda bi, j: (bi, 0, j)),
                  pl.BlockSpec((2 * n2, 2 * n2), const2),
                  pl.BlockSpec((2 * n2, 2 * n2), const2)],
        out_specs=pl.BlockSpec((None, 2 * n2, tn), lambda bi, j: (bi, 0, j)),
        out_shape=jax.ShapeDtypeStruct((b, 2 * n2, cols), F32),
        compiler_params=_cparams(("parallel", "parallel")),
        name="fft_stage2",
    )(y.reshape(b, 2 * n2, cols), t3h, t3l)

    tm = _tile(s, 512)
    gb = gate_col // c
    const3 = lambda bi, i: (0, 0, 0)
    return pl.pallas_call(
        _fftc_body,
        grid=(b, s // tm),
        in_specs=[pl.BlockSpec((None, 2, tm, c), lambda bi, i: (bi, 0, i, 0)),
                  pl.BlockSpec((2, c, c), const3),
                  pl.BlockSpec((2, c, c), const3),
                  pl.BlockSpec((c, c), lambda bi, i: (0, 0)),
                  pl.BlockSpec((None, tm, c), lambda bi, i: (bi, i, gb))],
        out_specs=pl.BlockSpec((None, tm, c), lambda bi, i: (bi, i, 0)),
        out_shape=jax.ShapeDtypeStruct((b, s, c), BF16),
        compiler_params=_cparams(("parallel", "parallel")),
        name="fft_channel",
    )(x.reshape(b, 2, s, c), tch, tcl, w_fnet, gates)


def _conv_body(zl_ref, zc_ref, zr_ref, cw_ref, cb_ref, lg_ref, lb_ref, w_ref, gate_ref, o_ref, ext_ref, *,
               c, ts, n_tiles):
    i = pl.program_id(1)

    def glu(z):
        return z[:, 0:c] * jax.nn.sigmoid(z[:, c:2 * c])

    left = jnp.where(i > 0, glu(zl_ref[...]), 0.0)
    right = jnp.where(i < n_tiles - 1, glu(zr_ref[...]), 0.0)
    ext_ref[0:CONV_HALO, :] = left
    ext_ref[CONV_HALO:CONV_HALO + ts, :] = glu(zc_ref[...])
    ext_ref[CONV_HALO + ts:2 * CONV_HALO + ts, :] = right
    off = CONV_HALO - CONV_W // 2
    acc = jnp.zeros((ts, c), F32) + cb_ref[...]
    for j in range(CONV_W):
        acc = acc + ext_ref[off + j:off + j + ts, :] * cw_ref[j:j + 1, :]
    mu = jnp.mean(acc, axis=-1, keepdims=True)
    d = acc - mu
    var = jnp.mean(d * d, axis=-1, keepdims=True)
    y = _silu(d * lax.rsqrt(var + EPS) * lg_ref[...] + lb_ref[...])
    out = _dot(y.astype(BF16), w_ref[...])
    o_ref[...] = (out * gate_ref[...].astype(F32)).astype(o_ref.dtype)


def _conv_branch(zglu, conv_w, conv_b, cln_g, cln_b, w_pw2, gates, gate_col):
    b, s, c2 = zglu.shape
    c = c2 // 2
    ts = _tile(s, 256, CONV_HALO)
    n_tiles = s // ts
    hb = ts // CONV_HALO
    n_hblk = s // CONV_HALO
    gb = gate_col // c
    row = lambda bi, i: (0, 0)
    return pl.pallas_call(
        functools.partial(_conv_body, c=c, ts=ts, n_tiles=n_tiles),
        grid=(b, n_tiles),
        in_specs=[pl.BlockSpec((None, CONV_HALO, c2), lambda bi, i: (bi, jnp.maximum(i * hb - 1, 0), 0)),
                  pl.BlockSpec((None, ts, c2), lambda bi, i: (bi, i, 0)),
                  pl.BlockSpec((None, CONV_HALO, c2),
                               lambda bi, i: (bi, jnp.minimum((i + 1) * hb, n_hblk - 1), 0)),
                  pl.BlockSpec((CONV_W, c), row),
                  pl.BlockSpec((1, c), row),
                  pl.BlockSpec((1, c), row),
                  pl.BlockSpec((1, c), row),
                  pl.BlockSpec((c, c), row),
                  pl.BlockSpec((None, ts, c), lambda bi, i: (bi, i, gb))],
        out_specs=pl.BlockSpec((None, ts, c), lambda bi, i: (bi, i, 0)),
        out_shape=jax.ShapeDtypeStruct((b, s, c), BF16),
        scratch_shapes=[pltpu.VMEM((ts + 2 * CONV_HALO, c), F32)],
        compiler_params=_cparams(("parallel", "arbitrary")),
        name="conv_branch",
    )(zglu, zglu, zglu, conv_w, conv_b, cln_g, cln_b, w_pw2, gates)


def _merge_body(yf_ref, ym_ref, yc_ref, wf_ref, wm_ref, wc_ref, gf_ref, gm_ref, gc_ref, o_ref):
    acc = gf_ref[...].astype(F32) * _dot(yf_ref[...], wf_ref[...])
    acc = acc + gm_ref[...].astype(F32) * _dot(ym_ref[...], wm_ref[...])
    acc = acc + gc_ref[...].astype(F32) * _dot(yc_ref[...], wc_ref[...])
    o_ref[...] = acc.astype(o_ref.dtype)


def _merge(yf, ym, yc, wf, wm, wc, g):
    m = yf.shape[0]
    d = wf.shape[1]
    tm = _tile(m, 512)
    tn = _tile(d, 1024, LANE)
    nj = d // tn
    yspec = lambda arr: pl.BlockSpec((tm, arr.shape[1]), lambda j, i: (i, 0))
    wspec = lambda arr: pl.BlockSpec((arr.shape[0], tn), lambda j, i: (0, j))
    gspec = lambda t: pl.BlockSpec((tm, tn), lambda j, i, t=t: (i, t * nj + j))
    return pl.pallas_call(
        _merge_body,
        grid=(nj, m // tm),
        in_specs=[yspec(yf), yspec(ym), yspec(yc), wspec(wf), wspec(wm), wspec(wc),
                  gspec(0), gspec(1), gspec(2)],
        out_specs=pl.BlockSpec((tm, tn), lambda j, i: (i, j)),
        out_shape=jax.ShapeDtypeStruct((m, d), BF16),
        compiler_params=_cparams(("parallel", "parallel")),
        name="merge",
    )(yf, ym, yc, wf, wm, wc, g, g, g)


def _rope_tables(n_tok):
    half = QK_ROPE // 4
    inv = ROPE_BASE ** (-jnp.arange(half, dtype=F32) / half)
    t = jnp.arange(n_tok, dtype=jnp.int32)
    rows = (t // GRID_W).astype(F32)[:, None] * inv[None, :]
    cols = (t % GRID_W).astype(F32)[:, None] * inv[None, :]
    cr, sr, cc, sc = jnp.cos(rows), jnp.sin(rows), jnp.cos(cols), jnp.sin(cols)
    pad1 = jnp.ones((n_tok, LANE - QK_ROPE), F32)
    pad0 = jnp.zeros((n_tok, LANE - QK_ROPE), F32)
    cos = jnp.concatenate([cr, cr, cc, cc, pad1], axis=1)
    sin = jnp.concatenate([-sr, sr, -sc, sc, pad0], axis=1)
    return cos, sin


def _swap_perm():
    q = QK_ROPE // 4
    return np.concatenate([np.arange(q, 2 * q), np.arange(0, q), np.arange(3 * q, 4 * q), np.arange(2 * q, 3 * q)])


def _pad_lanes(w, width=LANE):
    return jnp.pad(w, [(0, 0)] * (w.ndim - 1) + [(0, width - w.shape[-1])])


def _prep_layer(l, dims, w_in, w_uq, w_ukv, q_norm, k_norm, w_fnet, w_pw2, w_br_f, w_br_m, w_br_c, w_out):
    d, f, ql, kvl, h, c = dims
    mla = h * V_DIM
    off_fg = f
    off_q = 2 * f
    off_kv = off_q + ql
    off_kr = off_kv + kvl
    off_mg = off_kr + QK_ROPE
    off_glu = off_mg + mla
    off_cg = off_glu + 2 * c
    off_merge = off_cg + c
    perm = _swap_perm()
    wi = w_in[l]
    w_u = wi[:, 0:f].astype(BF16)
    w_fg = wi[:, off_fg:off_q].astype(BF16)
    w_q = wi[:, off_q:off_kv].astype(BF16)
    n_in = wi.shape[1]
    w_tail = _shift_cast(w_in, l, off_mg, n_in - off_mg)
    tail = dict(mg=(0, mla), glu=(off_glu - off_mg, 2 * c), cg=(off_cg - off_mg, c),
                merge=(off_merge - off_mg, n_in - off_merge))
    kr = wi[:, off_kr:off_mg]
    w_kv = jnp.concatenate([wi[:, off_kv:off_kr], _pad_lanes(kr), _pad_lanes(kr[:, perm])], axis=1).astype(BF16)

    wq3 = w_uq[l].reshape(ql, h, QK_DIM)
    q_nope = wq3[:, :, :QK_NOPE].reshape(ql, h * LANE)
    q_rope = wq3[:, :, QK_NOPE:]
    w_uq_p = jnp.concatenate([q_nope, _pad_lanes(q_rope).reshape(ql, h * LANE),
                              _pad_lanes(q_rope[:, :, perm]).reshape(ql, h * LANE)], axis=1).astype(BF16)
    wkv3 = w_ukv[l].reshape(kvl, h, QK_NOPE + V_DIM)
    w_ukv_p = jnp.concatenate([wkv3[:, :, :QK_NOPE].reshape(kvl, h * LANE),
                               wkv3[:, :, QK_NOPE:].reshape(kvl, h * LANE)], axis=1).astype(BF16)

    def gains(g):
        return (g[None, :QK_NOPE], _pad_lanes(g[None, QK_NOPE:]), _pad_lanes(g[None, QK_NOPE:][:, perm]))

    return dict(w_u=w_u, w_fg=w_fg, w_tail=w_tail, tail=tail, w_q=w_q, w_kv=w_kv,
                w_uq=w_uq_p, w_ukv=w_ukv_p, qg=gains(q_norm[l]), kg=gains(k_norm[l]),
                w_fnet=w_fnet[l].astype(BF16), w_pw2=w_pw2[l].astype(BF16),
                w_br_f=w_br_f[l].astype(BF16), w_br_m=w_br_m[l].astype(BF16), w_br_c=w_br_c[l].astype(BF16),
                w_out=w_out[l].astype(BF16))


def _kv_stream(h2, bsz, s, w, kv_a_norm, cos, sin, n_heads, kv_lora):
    zkv = _mm(h2, w["w_kv"], out_dtype=F32, tn=w["w_kv"].shape[1], name="in_proj_kv")
    gn, gr, gs = w["kg"]
    return _kvprep(zkv.reshape(bsz, s, -1), kv_a_norm, w["w_ukv"], gn, gr, gs, cos, sin, n_heads, kv_lora)


def _full_stream(x, h, kv_own, kv_ctx, gate, w, p, cos, sin, dims):
    d, f, ql, kvl, n_heads, c = dims
    bsz, s, _ = x.shape
    m = bsz * s
    h2 = h.reshape(m, d)
    zu = _mm(h2, w["w_u"], out_dtype=F32, name="in_proj_u")
    wt, tail = w["w_tail"], w["tail"]
    gate_f = _mm(h2, w["w_fg"], out_dtype=BF16, epi=_silu, name="in_proj_gate_f")
    gate_m = _mm(h2, wt, b_cols=tail["mg"], out_dtype=BF16, epi=_silu, name="in_proj_gate_m")
    gate_c = _mm(h2, wt, b_cols=tail["cg"], out_dtype=BF16, epi=_silu, name="in_proj_gate_c")
    gmerge = _mm(h2, wt, b_cols=tail["merge"], out_dtype=BF16, epi=jax.nn.sigmoid, name="in_proj_merge")
    zglu = _mm(h2, wt, b_cols=tail["glu"], out_dtype=F32, name="in_proj_glu")
    zq = _mm(h2, w["w_q"], out_dtype=F32, tn=ql, name="in_proj_q")

    gn, gr, gs = w["qg"]
    q = _qprep(zq.reshape(bsz, s, ql), p["q_a_norm"], w["w_uq"], gn, gr, gs, cos, sin, n_heads)
    sources = ([kv_ctx] if kv_ctx is not None else []) + [kv_own]
    y_m = _attention(q, sources, gate_m.reshape(bsz, s, -1), 0, n_heads)
    y_f = _fourier_branch(zu.reshape(bsz, s, f), w["w_fnet"], gate_f.reshape(bsz, s, f), 0)
    y_c = _conv_branch(zglu.reshape(bsz, s, 2 * c), p["conv_w"], p["conv_b"], p["cln_g"], p["cln_b"],
                       w["w_pw2"], gate_c.reshape(bsz, s, c), 0)
    merged = _merge(y_f.reshape(m, f), y_m.reshape(m, -1), y_c.reshape(m, c),
                    w["w_br_f"], w["w_br_m"], w["w_br_c"], gmerge)
    out = _mm(merged, w["w_out"], out_dtype=F32, rows_per_batch=s,
              epi=lambda acc, xv, gv: xv + gv * acc,
              extras=(("mn", x.reshape(m, d)), ("bn", gate)), name="out_proj")
    return out.reshape(bsz, s, d)


def kernel(x, c, ctx, c_ctx, norm_g, w_ada, b_ada, w_in, q_a_norm, w_uq, kv_a_norm, w_ukv, q_norm, k_norm,
           w_fnet, conv_w, conv_b, cln_g, cln_b, w_pw2, w_br_f, w_br_m, w_br_c, w_out):
    bsz, n_tok, d = x.shape
    n_ctx = ctx.shape[1]
    depth = w_in.shape[0]
    f = w_fnet.shape[1]
    ql = q_a_norm.shape[1]
    kvl = kv_a_norm.shape[1]
    n_heads = w_uq.shape[2] // QK_DIM
    cdim = conv_b.shape[1]
    dims = (d, f, ql, kvl, n_heads, cdim)

    cos_l, sin_l = _rope_tables(n_tok)
    cos_c = jnp.ones((n_ctx, LANE), F32)
    sin_c = jnp.zeros((n_ctx, LANE), F32)

    n_rows = -(-(bsz + 1) // 8) * 8
    cond = jnp.concatenate([c, c_ctx[None, :], jnp.zeros((n_rows - bsz - 1, d), F32)], axis=0)

    xl, xc = x, ctx
    for l in range(depth):
        last = l == depth - 1
        w = _prep_layer(l, dims, w_in, w_uq, w_ukv, q_norm, k_norm, w_fnet, w_pw2, w_br_f, w_br_m, w_br_c, w_out)
        p = dict(q_a_norm=q_a_norm[l][None], conv_w=conv_w[l], conv_b=conv_b[l][None],
                 cln_g=cln_g[l][None], cln_b=cln_b[l][None])
        mod = _mm(cond, w_ada[l], out_dtype=F32, tm=n_rows, tn=1024, tk=2048, a_act=_silu,
                  epi=lambda acc, bias: acc + bias, extras=(("n", b_ada[l][None]),), name="adaln")
        shift_l, scale_l, gate_l = (mod[:bsz, i * d:(i + 1) * d][:, None, :] for i in range(3))
        shift_c, scale_c, gate_c = (jnp.broadcast_to(mod[bsz, i * d:(i + 1) * d][None, None, :], (bsz, 1, d))
                                    for i in range(3))
        g = norm_g[l][None]
        hl = _norm_mod(xl, g, scale_l, shift_l)
        hc = _norm_mod(xc, g, scale_c, shift_c)
        kv_c = _kv_stream(hc.reshape(bsz * n_ctx, d), bsz, n_ctx, w, kv_a_norm[l][None], cos_c, sin_c, n_heads, kvl)
        kv_l = _kv_stream(hl.reshape(bsz * n_tok, d), bsz, n_tok, w, kv_a_norm[l][None], cos_l, sin_l, n_heads, kvl)
        new_xl = _full_stream(xl, hl, kv_l, kv_c, gate_l, w, p, cos_l, sin_l, dims)
        if not last:
            xc = _full_stream(xc, hc, kv_c, None, gate_c, w, p, cos_c, sin_c, dims)
        xl = new_xl
    return xl
```

```python
import functools
import math

import numpy as np
import jax
import jax.numpy as jnp
from jax import lax
from jax.experimental import pallas as pl
from jax.experimental.pallas import tpu as pltpu

F32 = jnp.float32
BF16 = jnp.bfloat16

EPS = 1e-6
QK_NOPE = 128
QK_ROPE = 64
V_DIM = 128
QK_DIM = QK_NOPE + QK_ROPE
HEAD_PAD = 256
LANE = 128
SUBLANE = 8
GRID_W = 64
CONV_W = 31
CONV_HALO = 16
ROPE_BASE = 10000.0
N_BRANCH = 3
ATTN_TQ = 512
ATTN_TK = 2048
VMEM_LIMIT = 52 * 1024 * 1024


def _cparams(sem):
    return pltpu.CompilerParams(dimension_semantics=sem, vmem_limit_bytes=VMEM_LIMIT)


def _tile(n, pref, align=8):
    if n <= pref:
        return n
    t = (pref // align) * align
    while t >= align:
        if n % t == 0:
            return t
        t -= align
    return n


def _sigmoid(v):
    return 0.5 * jnp.tanh(0.5 * v) + 0.5


def _silu(v):
    return v * _sigmoid(v)


def _split_hi_lo(v):
    hi = v.astype(BF16)
    lo = (v - hi.astype(F32)).astype(BF16)
    return hi, lo


def _dot(a, b):
    return jnp.dot(a, b, preferred_element_type=F32)


def _dot3(a_hi, a_lo, b_hi, b_lo):
    return _dot(a_hi, b_hi) + (_dot(a_hi, b_lo) + _dot(a_lo, b_hi))


MM_SUB = 256


def _mm_body(*refs, nk, n_extra, epi, a_act, b_t):
    a_ref, b_ref = refs[0], refs[1]
    extra = refs[2:2 + n_extra]
    o_ref = refs[2 + n_extra]
    a = a_ref[...]
    if a_act is not None:
        a = a_act(a.astype(F32))
    a = a.astype(BF16)
    tn = o_ref.shape[-1]

    def prod(c0, width):
        if b_t:
            return lax.dot_general(a, b_ref[c0:c0 + width, :].astype(BF16), (((1,), (1,)), ((), ())),
                                   preferred_element_type=F32)
        return _dot(a, b_ref[:, c0:c0 + width].astype(BF16))

    if nk == 1:
        sub = MM_SUB if tn % MM_SUB == 0 else tn
        for c0 in range(0, tn, sub):
            p = prod(c0, sub)
            o_ref[:, c0:c0 + sub] = epi(p, *[e[:, c0:c0 + sub] for e in extra]).astype(o_ref.dtype)
    else:
        p = prod(0, tn)
        acc_ref = refs[3 + n_extra]
        k = pl.program_id(2)

        @pl.when(k == 0)
        def _():
            acc_ref[...] = p

        @pl.when(k > 0)
        def _():
            acc_ref[...] += p

        @pl.when(k == nk - 1)
        def _():
            o_ref[...] = epi(acc_ref[...], *[e[...] for e in extra]).astype(o_ref.dtype)


def _mm(a, b, *, out_dtype, tm=1024, tn=512, tk=4096, epi=None, extras=(), a_act=None,
        rows_per_batch=None, b_cols=None, b_layer=None, b_t=False, name="mm"):
    m, kdim = a.shape
    col0, n = (0, b.shape[-2 if b_t else -1]) if b_cols is None else b_cols
    tm = _tile(m, tm)
    if rows_per_batch is not None:
        tm = _tile(rows_per_batch, tm)
    tn = _tile(n, tn, LANE)
    while col0 % tn:
        tn = _tile(n, tn - LANE, LANE)
    tk = _tile(kdim, tk, LANE)
    nk = kdim // tk
    jb = col0 // tn
    if epi is None:
        epi = lambda p: p
    lead = (None,) if b.ndim == 3 else ()
    lidx = (b_layer,) if b.ndim == 3 else ()
    if b_t:
        b_spec = pl.BlockSpec(lead + (tn, tk), lambda i, j, k: lidx + (j + jb, k))
    else:
        b_spec = pl.BlockSpec(lead + (tk, tn), lambda i, j, k: lidx + (k, j + jb))
    in_specs = [pl.BlockSpec((tm, tk), lambda i, j, k: (i, k)), b_spec]
    args = [a, b]
    for kind, arr in extras:
        if kind == "mn":
            in_specs.append(pl.BlockSpec((tm, tn), lambda i, j, k: (i, j)))
        elif kind == "n":
            in_specs.append(pl.BlockSpec((1, tn), lambda i, j, k: (0, j)))
        else:
            bpt = rows_per_batch // tm
            in_specs.append(pl.BlockSpec((None, 1, tn), lambda i, j, k, bpt=bpt: (i // bpt, 0, j)))
        args.append(arr)
    scratch = [pltpu.VMEM((tm, tn), F32)] if nk > 1 else []
    return pl.pallas_call(
        functools.partial(_mm_body, nk=nk, n_extra=len(extras), epi=epi, a_act=a_act, b_t=b_t),
        grid=(m // tm, n // tn, nk),
        in_specs=in_specs,
        out_specs=pl.BlockSpec((tm, tn), lambda i, j, k: (i, j)),
        out_shape=jax.ShapeDtypeStruct((m, n), out_dtype),
        scratch_shapes=scratch,
        compiler_params=_cparams(("parallel", "parallel", "arbitrary")),
        name=name,
    )(*args)


def _cast_rows_body(w_ref, o_ref):
    o_ref[...] = w_ref[...].astype(o_ref.dtype)


def _cast_rows(w, l, start, n_rows):
    kdim = w.shape[2]
    rb = 512
    while start % rb or n_rows % rb:
        rb //= 2
    jb = start // rb
    return pl.pallas_call(
        _cast_rows_body,
        grid=(n_rows // rb,),
        in_specs=[pl.BlockSpec((None, rb, kdim), lambda j: (l, jb + j, 0))],
        out_specs=pl.BlockSpec((rb, kdim), lambda j: (j, 0)),
        out_shape=jax.ShapeDtypeStruct((n_rows, kdim), BF16),
        compiler_params=_cparams(("parallel",)),
        name="cast_rows",
    )(w)


def _norm_body(x_ref, g_ref, scale_ref, shift_ref, o_ref):
    x = x_ref[...]
    r = lax.rsqrt(jnp.mean(x * x, axis=-1, keepdims=True) + EPS)
    y = x * r * g_ref[...]
    o_ref[...] = (y * (1.0 + scale_ref[...]) + shift_ref[...]).astype(o_ref.dtype)


def _norm_mod(x, g, scale, shift):
    b, s, d = x.shape
    tr = _tile(s, 256)
    return pl.pallas_call(
        _norm_body,
        grid=(b, s // tr),
        in_specs=[pl.BlockSpec((None, tr, d), lambda bi, i: (bi, i, 0)),
                  pl.BlockSpec((1, d), lambda bi, i: (0, 0)),
                  pl.BlockSpec((None, 1, d), lambda bi, i: (bi, 0, 0)),
                  pl.BlockSpec((None, 1, d), lambda bi, i: (bi, 0, 0))],
        out_specs=pl.BlockSpec((None, tr, d), lambda bi, i: (bi, i, 0)),
        out_shape=jax.ShapeDtypeStruct((b, s, d), BF16),
        compiler_params=_cparams(("parallel", "parallel")),
        name="norm_mod",
    )(x, g, scale, shift)


def _qprep_body(z_ref, an_ref, w_ref, gn_ref, gr_ref, gs_ref, cos_ref, sin_ref, q_ref, *, n_heads, scale):
    x = z_ref[...]
    r = lax.rsqrt(jnp.mean(x * x, axis=-1, keepdims=True) + EPS)
    xn = (x * r * an_ref[...]).astype(BF16)
    acc = _dot(xn, w_ref[...])
    cos = cos_ref[...] * gr_ref[...]
    sin = sin_ref[...] * gs_ref[...]
    gn = gn_ref[...]
    hw = n_heads * LANE
    for h in range(n_heads):
        nope = acc[:, h * LANE:(h + 1) * LANE]
        rp = acc[:, hw + h * LANE: hw + (h + 1) * LANE]
        rs = acc[:, 2 * hw + h * LANE: 2 * hw + (h + 1) * LANE]
        ss = jnp.sum(nope * nope, axis=-1, keepdims=True) + jnp.sum(rp * rp, axis=-1, keepdims=True)
        inv = lax.rsqrt(ss * (1.0 / QK_DIM) + EPS) * scale
        q_ref[h, :, 0:LANE] = (nope * inv * gn).astype(q_ref.dtype)
        q_ref[h, :, LANE:HEAD_PAD] = ((rp * cos + rs * sin) * inv).astype(q_ref.dtype)


def _qprep(zq, a_norm, w_p, gn, gr, gs, cos, sin, n_heads):
    b, s, ql = zq.shape
    tr = _tile(s, 256)
    nw = w_p.shape[1]
    return pl.pallas_call(
        functools.partial(_qprep_body, n_heads=n_heads, scale=QK_DIM ** -0.5),
        grid=(b, s // tr),
        in_specs=[pl.BlockSpec((None, tr, ql), lambda bi, i: (bi, i, 0)),
                  pl.BlockSpec((1, ql), lambda bi, i: (0, 0)),
                  pl.BlockSpec((ql, nw), lambda bi, i: (0, 0)),
                  pl.BlockSpec((1, LANE), lambda bi, i: (0, 0)),
                  pl.BlockSpec((1, LANE), lambda bi, i: (0, 0)),
                  pl.BlockSpec((1, LANE), lambda bi, i: (0, 0)),
                  pl.BlockSpec((tr, LANE), lambda bi, i: (i, 0)),
                  pl.BlockSpec((tr, LANE), lambda bi, i: (i, 0))],
        out_specs=pl.BlockSpec((None, n_heads, tr, HEAD_PAD), lambda bi, i: (bi, 0, i, 0)),
        out_shape=jax.ShapeDtypeStruct((b, n_heads, s, HEAD_PAD), BF16),
        compiler_params=_cparams(("parallel", "parallel")),
        name="q_prep",
    )(zq, a_norm, w_p, gn, gr, gs, cos, sin)


def _kvprep_body(z_ref, an_ref, w_ref, gn_ref, gr_ref, gs_ref, cos_ref, sin_ref, k_ref, v_ref, *,
                 n_heads, kv_lora):
    x = z_ref[:, 0:kv_lora]
    tr = x.shape[0]
    lane = lax.broadcasted_iota(jnp.int32, (tr, LANE), 1)
    kr = jnp.where(lane < QK_ROPE, z_ref[:, kv_lora:kv_lora + LANE], 0.0)
    quarter = QK_ROPE // 4
    krs = jnp.where((lane // quarter) % 2 == 0,
                    pltpu.roll(kr, LANE - quarter, axis=1), pltpu.roll(kr, quarter, axis=1))
    r = lax.rsqrt(jnp.mean(x * x, axis=-1, keepdims=True) + EPS)
    xn = (x * r * an_ref[...]).astype(BF16)
    acc = _dot(xn, w_ref[...])
    ss_r = jnp.sum(kr * kr, axis=-1, keepdims=True)
    rope = kr * (cos_ref[...] * gr_ref[...]) + krs * (sin_ref[...] * gs_ref[...])
    gn = gn_ref[...]
    hw = n_heads * LANE
    ones_col = (lane == 0).astype(v_ref.dtype)
    for h in range(n_heads):
        kn = acc[:, h * LANE:(h + 1) * LANE]
        ss = jnp.sum(kn * kn, axis=-1, keepdims=True) + ss_r
        inv = lax.rsqrt(ss * (1.0 / QK_DIM) + EPS)
        k_ref[h, :, 0:LANE] = (kn * inv * gn).astype(k_ref.dtype)
        k_ref[h, :, LANE:HEAD_PAD] = (rope * inv).astype(k_ref.dtype)
        v_ref[h, :, 0:LANE] = acc[:, hw + h * LANE: hw + (h + 1) * LANE].astype(v_ref.dtype)
        v_ref[h, :, LANE:HEAD_PAD] = ones_col


def _kvprep(zkv, a_norm, w_p, gn, gr, gs, cos, sin, n_heads, kv_lora):
    b, s, zw = zkv.shape
    tr = _tile(s, 256)
    nw = w_p.shape[1]
    out = jax.ShapeDtypeStruct((b, n_heads, s, HEAD_PAD), BF16)
    ospec = pl.BlockSpec((None, n_heads, tr, HEAD_PAD), lambda bi, i: (bi, 0, i, 0))
    return pl.pallas_call(
        functools.partial(_kvprep_body, n_heads=n_heads, kv_lora=kv_lora),
        grid=(b, s // tr),
        in_specs=[pl.BlockSpec((None, tr, zw), lambda bi, i: (bi, i, 0)),
                  pl.BlockSpec((1, kv_lora), lambda bi, i: (0, 0)),
                  pl.BlockSpec((kv_lora, nw), lambda bi, i: (0, 0)),
                  pl.BlockSpec((1, LANE), lambda bi, i: (0, 0)),
                  pl.BlockSpec((1, LANE), lambda bi, i: (0, 0)),
                  pl.BlockSpec((1, LANE), lambda bi, i: (0, 0)),
                  pl.BlockSpec((tr, LANE), lambda bi, i: (i, 0)),
                  pl.BlockSpec((tr, LANE), lambda bi, i: (i, 0))],
        out_specs=[ospec, ospec],
        out_shape=[out, out],
        compiler_params=_cparams(("parallel", "parallel")),
        name="kv_prep",
    )(zkv, a_norm, w_p, gn, gr, gs, cos, sin)


def _attn_body(*refs, n_src, chunks):
    q_ref = refs[0]
    kv = refs[1:1 + 2 * n_src]
    gate_ref = refs[1 + 2 * n_src]
    o_ref = refs[2 + 2 * n_src]
    q = q_ref[...]
    tq = q.shape[0]
    plan = []
    for si in range(n_src):
        tk, n_chunks = chunks[si]
        plan += [(kv[2 * si], kv[2 * si + 1], ci * tk, tk) for ci in range(n_chunks)]

    def scores(i):
        k_ref, _, start, tk = plan[i]
        return lax.dot_general(q, k_ref[start:start + tk, :], (((1,), (1,)), ((), ())),
                               preferred_element_type=F32)

    m = jnp.full((tq, LANE), -jnp.inf, F32)
    acc = jnp.zeros((tq, HEAD_PAD), F32)
    s_next = scores(0)
    for i in range(len(plan)):
        s = s_next
        if i + 1 < len(plan):
            s_next = scores(i + 1)
        _, v_ref, start, tk = plan[i]
        m_new = jnp.maximum(m, jnp.max(s, axis=-1, keepdims=True))
        p = jnp.exp((s - jnp.tile(m_new, (1, tk // LANE))).astype(BF16))
        alpha = jnp.exp(m - m_new)
        acc = acc * jnp.tile(alpha, (1, HEAD_PAD // LANE)) + _dot(p, v_ref[start:start + tk, :])
        m = m_new
    out = acc[:, 0:V_DIM] / acc[:, V_DIM:V_DIM + 1]
    o_ref[...] = (out * gate_ref[...].astype(F32)).astype(o_ref.dtype)


def _attention(q, sources, gates, gate_col, n_heads):
    b, _, s, _ = q.shape
    tq = _tile(s, ATTN_TQ)
    in_specs = [pl.BlockSpec((None, None, tq, HEAD_PAD), lambda bi, h, i: (bi, h, i, 0))]
    args = [q]
    chunks = []
    for k, v in sources:
        sk = k.shape[2]
        tk = _tile(sk, ATTN_TK)
        chunks.append((tk, sk // tk))
        spec = pl.BlockSpec((None, None, sk, HEAD_PAD), lambda bi, h, i: (bi, h, 0, 0))
        in_specs += [spec, spec]
        args += [k, v]
    gb = gate_col // V_DIM
    in_specs.append(pl.BlockSpec((None, tq, V_DIM), lambda bi, h, i: (bi, i, gb + h)))
    args.append(gates)
    return pl.pallas_call(
        functools.partial(_attn_body, n_src=len(sources), chunks=tuple(chunks)),
        grid=(b, n_heads, s // tq),
        in_specs=in_specs,
        out_specs=pl.BlockSpec((None, tq, V_DIM), lambda bi, h, i: (bi, i, h)),
        out_shape=jax.ShapeDtypeStruct((b, s, n_heads * V_DIM), BF16),
        compiler_params=_cparams(("parallel", "parallel", "arbitrary")),
        name="attention",
    )(*args)


def _fft_factors(s):
    n1 = 128 if s % 128 == 0 and s // 128 >= 16 else 16
    return n1, s // n1


@functools.lru_cache(maxsize=None)
def _fft_tables(s, c):
    n1, n2 = _fft_factors(s)
    k1 = np.arange(n1, dtype=np.float64)
    a1 = 2.0 * np.pi * np.outer(k1, k1) / n1
    t1 = np.concatenate([np.cos(a1), -np.sin(a1)], axis=0)
    at = 2.0 * np.pi * np.outer(np.arange(n2, dtype=np.float64), k1) / s
    twr = np.cos(at)[:, :, None]
    twi = (-np.sin(at))[:, :, None]
    k2 = np.arange(n2, dtype=np.float64)
    a3 = 2.0 * np.pi * np.outer(k2, k2) / n2
    c3, s3 = np.cos(a3), np.sin(a3)
    t3 = np.block([[c3, s3], [-s3, c3]])
    kc = np.arange(c, dtype=np.float64)
    ac = 2.0 * np.pi * np.outer(kc, kc) / c
    tc = np.stack([np.cos(ac), np.sin(ac)], axis=0) / math.sqrt(float(s) * float(c))

    def hl(t):
        t32 = t.astype(np.float32)
        hi = t32.astype(BF16)
        lo = (t32 - hi.astype(np.float32)).astype(BF16)
        return hi, lo

    return dict(t1=hl(t1), twr=twr.astype(np.float32), twi=twi.astype(np.float32), t3=hl(t3), tc=hl(tc))


def _fft1_body(u_ref, thi_ref, tlo_ref, twr_ref, twi_ref, y_ref, *, n1):
    u_hi, u_lo = _split_hi_lo(u_ref[...])
    y = _dot3(thi_ref[...], tlo_ref[...], u_hi, u_lo)
    yr, yi = y[0:n1], y[n1:2 * n1]
    tr, ti = twr_ref[...], twi_ref[...]
    y_ref[0] = yr * tr - yi * ti
    y_ref[1] = yr * ti + yi * tr


def _fft3_body(y_ref, thi_ref, tlo_ref, x_ref):
    y_hi, y_lo = _split_hi_lo(y_ref[...])
    x_ref[...] = _dot3(thi_ref[...], tlo_ref[...], y_hi, y_lo)


def _fftc_body(x_ref, thi_ref, tlo_ref, wf_ref, gate_ref, o_ref):
    xr_hi, xr_lo = _split_hi_lo(x_ref[0])
    xi_hi, xi_lo = _split_hi_lo(x_ref[1])
    z = _dot3(xr_hi, xr_lo, thi_ref[0], tlo_ref[0]) + _dot3(xi_hi, xi_lo, thi_ref[1], tlo_ref[1])
    y = _dot(z.astype(BF16), wf_ref[...])
    o_ref[...] = (y * gate_ref[...].astype(F32)).astype(o_ref.dtype)


def _fourier_branch(u, w_fnet, gates, gate_col):
    b, s, c = u.shape
    n1, n2 = _fft_factors(s)
    tabs = _fft_tables(s, c)
    bf = lambda t: jnp.asarray(t, dtype=BF16)
    t1h, t1l = map(bf, tabs["t1"])
    t3h, t3l = map(bf, tabs["t3"])
    tch, tcl = map(bf, tabs["tc"])
    twr, twi = jnp.asarray(tabs["twr"]), jnp.asarray(tabs["twi"])

    const2 = lambda bi, j: (0, 0)
    y = pl.pallas_call(
        functools.partial(_fft1_body, n1=n1),
        grid=(b, n2),
        in_specs=[pl.BlockSpec((None, n1, c), lambda bi, j: (bi, 0, j)),
                  pl.BlockSpec((2 * n1, n1), const2),
                  pl.BlockSpec((2 * n1, n1), const2),
                  pl.BlockSpec((None, n1, 1), lambda bi, j: (j, 0, 0)),
                  pl.BlockSpec((None, n1, 1), lambda bi, j: (j, 0, 0))],
        out_specs=pl.BlockSpec((None, 2, None, n1, c), lambda bi, j: (bi, 0, j, 0, 0)),
        out_shape=jax.ShapeDtypeStruct((b, 2, n2, n1, c), F32),
        compiler_params=_cparams(("parallel", "parallel")),
        name="fft_stage1",
    )(u.reshape(b, n1, n2 * c), t1h, t1l, twr, twi)

    cols = n1 * c
    tn = _tile(cols, 4096, LANE)
    x = pl.pallas_call(
        _fft3_body,
        grid=(b, cols // tn),
        in_specs=[pl.BlockSpec((None, 2 * n2, tn), lambda bi, j: (bi, 0, j)),
                  pl.BlockSpec((2 * n2, 2 * n2), const2),
                  pl.BlockSpec((2 * n2, 2 * n2), const2)],
        out_specs=pl.BlockSpec((None, 2 * n2, tn), lambda bi, j: (bi, 0, j)),
        out_shape=jax.ShapeDtypeStruct((b, 2 * n2, cols), F32),
        compiler_params=_cparams(("parallel", "parallel")),
        name="fft_stage2",
    )(y.reshape(b, 2 * n2, cols), t3h, t3l)

    tm = _tile(s, 512)
    gb = gate_col // c
    const3 = lambda bi, i: (0, 0, 0)
    return pl.pallas_call(
        _fftc_body,
        grid=(b, s // tm),
        in_specs=[pl.BlockSpec((None, 2, tm, c), lambda bi, i: (bi, 0, i, 0)),
                  pl.BlockSpec((2, c, c), const3),
                  pl.BlockSpec((2, c, c), const3),
                  pl.BlockSpec((c, c), lambda bi, i: (0, 0)),
                  pl.BlockSpec((None, tm, c), lambda bi, i: (bi, i, gb))],
        out_specs=pl.BlockSpec((None, tm, c), lambda bi, i: (bi, i, 0)),
        out_shape=jax.ShapeDtypeStruct((b, s, c), BF16),
        compiler_params=_cparams(("parallel", "parallel")),
        name="fft_channel",
    )(x.reshape(b, 2, s, c), tch, tcl, w_fnet, gates)


def _conv_body(zl_ref, zc_ref, zr_ref, cw_ref, cb_ref, lg_ref, lb_ref, w_ref, gate_ref, o_ref, ext_ref, sh_ref,
               *, c, ts, n_tiles):
    i = pl.program_id(1)

    def glu(z):
        return z[:, 0:c] * _sigmoid(z[:, c:2 * c])

    left = jnp.where(i > 0, glu(zl_ref[...]), 0.0)
    right = jnp.where(i < n_tiles - 1, glu(zr_ref[...]), 0.0)
    ext_ref[0:CONV_HALO, :] = left
    ext_ref[CONV_HALO:CONV_HALO + ts, :] = glu(zc_ref[...])
    ext_ref[CONV_HALO + ts:2 * CONV_HALO + ts, :] = right
    span = sh_ref.shape[1]
    for r in range(1, SUBLANE):
        sh_ref[r - 1] = ext_ref[r:r + span, :]
    off = CONV_HALO - CONV_W // 2
    acc = jnp.zeros((ts, c), F32) + cb_ref[...]
    for j in range(CONV_W):
        r, base = (off + j) % SUBLANE, (off + j) // SUBLANE * SUBLANE
        src = ext_ref[base:base + ts, :] if r == 0 else sh_ref[r - 1, base:base + ts, :]
        acc = acc + src * cw_ref[j:j + 1, :]
    mu = jnp.mean(acc, axis=-1, keepdims=True)
    d = acc - mu
    var = jnp.mean(d * d, axis=-1, keepdims=True)
    y = _silu(d * lax.rsqrt(var + EPS) * lg_ref[...] + lb_ref[...])
    out = _dot(y.astype(BF16), w_ref[...])
    o_ref[...] = (out * gate_ref[...].astype(F32)).astype(o_ref.dtype)


def _conv_branch(zglu, conv_w, conv_b, cln_g, cln_b, w_pw2, gates, gate_col):
    b, s, c2 = zglu.shape
    c = c2 // 2
    ts = _tile(s, 256, CONV_HALO)
    n_tiles = s // ts
    hb = ts // CONV_HALO
    n_hblk = s // CONV_HALO
    gb = gate_col // c
    row = lambda bi, i: (0, 0)
    return pl.pallas_call(
        functools.partial(_conv_body, c=c, ts=ts, n_tiles=n_tiles),
        grid=(b, n_tiles),
        in_specs=[pl.BlockSpec((None, CONV_HALO, c2), lambda bi, i: (bi, jnp.maximum(i * hb - 1, 0), 0)),
                  pl.BlockSpec((None, ts, c2), lambda bi, i: (bi, i, 0)),
                  pl.BlockSpec((None, CONV_HALO, c2),
                               lambda bi, i: (bi, jnp.minimum((i + 1) * hb, n_hblk - 1), 0)),
                  pl.BlockSpec((CONV_W, c), row),
                  pl.BlockSpec((1, c), row),
                  pl.BlockSpec((1, c), row),
                  pl.BlockSpec((1, c), row),
                  pl.BlockSpec((c, c), row),
                  pl.BlockSpec((None, ts, c), lambda bi, i: (bi, i, gb))],
        out_specs=pl.BlockSpec((None, ts, c), lambda bi, i: (bi, i, 0)),
        out_shape=jax.ShapeDtypeStruct((b, s, c), BF16),
        scratch_shapes=[pltpu.VMEM((ts + 2 * CONV_HALO, c), F32),
                        pltpu.VMEM((SUBLANE - 1, ts + 2 * CONV_HALO - SUBLANE, c), F32)],
        compiler_params=_cparams(("parallel", "arbitrary")),
        name="conv_branch",
    )(zglu, zglu, zglu, conv_w, conv_b, cln_g, cln_b, w_pw2, gates)


def _merge_body(yf_ref, ym_ref, yc_ref, wf_ref, wm_ref, wc_ref, gf_ref, gm_ref, gc_ref, o_ref):
    acc = gf_ref[...].astype(F32) * _dot(yf_ref[...], wf_ref[...])
    acc = acc + gm_ref[...].astype(F32) * _dot(ym_ref[...], wm_ref[...])
    acc = acc + gc_ref[...].astype(F32) * _dot(yc_ref[...], wc_ref[...])
    o_ref[...] = acc.astype(o_ref.dtype)


def _merge(yf, ym, yc, wf, wm, wc, g):
    m = yf.shape[0]
    d = wf.shape[1]
    tm = _tile(m, 512)
    tn = _tile(d, 1024, LANE)
    nj = d // tn
    yspec = lambda arr: pl.BlockSpec((tm, arr.shape[1]), lambda j, i: (i, 0))
    wspec = lambda arr: pl.BlockSpec((arr.shape[0], tn), lambda j, i: (0, j))
    gspec = lambda t: pl.BlockSpec((tm, tn), lambda j, i, t=t: (i, t * nj + j))
    return pl.pallas_call(
        _merge_body,
        grid=(nj, m // tm),
        in_specs=[yspec(yf), yspec(ym), yspec(yc), wspec(wf), wspec(wm), wspec(wc),
                  gspec(0), gspec(1), gspec(2)],
        out_specs=pl.BlockSpec((tm, tn), lambda j, i: (i, j)),
        out_shape=jax.ShapeDtypeStruct((m, d), BF16),
        compiler_params=_cparams(("parallel", "parallel")),
        name="merge",
    )(yf, ym, yc, wf, wm, wc, g, g, g)


def _rope_tables(n_tok):
    half = QK_ROPE // 4
    inv = ROPE_BASE ** (-jnp.arange(half, dtype=F32) / half)
    t = jnp.arange(n_tok, dtype=jnp.int32)
    rows = (t // GRID_W).astype(F32)[:, None] * inv[None, :]
    cols = (t % GRID_W).astype(F32)[:, None] * inv[None, :]
    cr, sr, cc, sc = jnp.cos(rows), jnp.sin(rows), jnp.cos(cols), jnp.sin(cols)
    pad1 = jnp.ones((n_tok, LANE - QK_ROPE), F32)
    pad0 = jnp.zeros((n_tok, LANE - QK_ROPE), F32)
    cos = jnp.concatenate([cr, cr, cc, cc, pad1], axis=1)
    sin = jnp.concatenate([-sr, sr, -sc, sc, pad0], axis=1)
    return cos, sin


def _swap_perm():
    q = QK_ROPE // 4
    return np.concatenate([np.arange(q, 2 * q), np.arange(0, q), np.arange(3 * q, 4 * q), np.arange(2 * q, 3 * q)])


def _pad_lanes(w, width=LANE):
    return jnp.pad(w, [(0, 0)] * (w.ndim - 1) + [(0, width - w.shape[-1])])


def _prep_layer(l, dims, w_in, w_uq, w_ukv, q_norm, k_norm, w_fnet, w_pw2, w_br_f, w_br_m, w_br_c, w_out):
    d, f, ql, kvl, h, c = dims
    mla = h * V_DIM
    off_fg = f
    off_q = 2 * f
    off_kv = off_q + ql
    off_kr = off_kv + kvl
    off_mg = off_kr + QK_ROPE
    off_glu = off_mg + mla
    off_cg = off_glu + 2 * c
    off_merge = off_cg + c
    perm = _swap_perm()
    head = dict(u=(0, f), fg=(off_fg, f), q=(off_q, ql), kv=(off_kv, kvl + LANE))
    n_in = w_in.shape[1]
    w_tail = _cast_rows(w_in, l, off_mg, n_in - off_mg)
    tail = dict(mg=(0, mla), glu=(off_glu - off_mg, 2 * c), cg=(off_cg - off_mg, c),
                merge=(off_merge - off_mg, n_in - off_merge))

    wq3 = w_uq[l].reshape(ql, h, QK_DIM)
    q_nope = wq3[:, :, :QK_NOPE].reshape(ql, h * LANE)
    q_rope = wq3[:, :, QK_NOPE:]
    w_uq_p = jnp.concatenate([q_nope, _pad_lanes(q_rope).reshape(ql, h * LANE),
                              _pad_lanes(q_rope[:, :, perm]).reshape(ql, h * LANE)], axis=1).astype(BF16)
    wkv3 = w_ukv[l].reshape(kvl, h, QK_NOPE + V_DIM)
    w_ukv_p = jnp.concatenate([wkv3[:, :, :QK_NOPE].reshape(kvl, h * LANE),
                               wkv3[:, :, QK_NOPE:].reshape(kvl, h * LANE)], axis=1).astype(BF16)

    def gains(g):
        return (g[None, :QK_NOPE], _pad_lanes(g[None, QK_NOPE:]), _pad_lanes(g[None, QK_NOPE:][:, perm]))

    return dict(l=l, w_in=w_in, head=head, w_tail=w_tail, tail=tail,
                w_uq=w_uq_p, w_ukv=w_ukv_p, qg=gains(q_norm[l]), kg=gains(k_norm[l]),
                w_fnet=w_fnet[l].astype(BF16), w_pw2=w_pw2[l].astype(BF16),
                w_br_f=w_br_f[l].astype(BF16), w_br_m=w_br_m[l].astype(BF16), w_br_c=w_br_c[l].astype(BF16),
                w_out=w_out[l].astype(BF16))


def _kv_stream(h2, bsz, s, w, kv_a_norm, cos, sin, n_heads, kv_lora):
    zkv = _mm(h2, w["w_in"], b_t=True, b_layer=w["l"], b_cols=w["head"]["kv"], out_dtype=F32,
              tn=kv_lora + LANE, name="in_proj_kv")
    gn, gr, gs = w["kg"]
    return _kvprep(zkv.reshape(bsz, s, -1), kv_a_norm, w["w_ukv"], gn, gr, gs, cos, sin, n_heads, kv_lora)


def _full_stream(x, h, kv_own, kv_ctx, gate, w, p, cos, sin, dims):
    d, f, ql, kvl, n_heads, c = dims
    bsz, s, _ = x.shape
    m = bsz * s
    h2 = h.reshape(m, d)
    wi, li, head = w["w_in"], w["l"], w["head"]
    wt, tail = w["w_tail"], w["tail"]
    nt = dict(b_t=True)
    zu = _mm(h2, wi, b_layer=li, b_cols=head["u"], out_dtype=F32, name="in_proj_u", **nt)
    gate_f = _mm(h2, wi, b_layer=li, b_cols=head["fg"], out_dtype=BF16, epi=_silu, name="in_proj_gate_f", **nt)
    zq = _mm(h2, wi, b_layer=li, b_cols=head["q"], out_dtype=F32, name="in_proj_q", **nt)
    gate_m = _mm(h2, wt, b_cols=tail["mg"], tn=1024, out_dtype=BF16, epi=_silu, name="in_proj_gate_m", **nt)
    gate_c = _mm(h2, wt, b_cols=tail["cg"], tn=1024, out_dtype=BF16, epi=_silu, name="in_proj_gate_c", **nt)
    gmerge = _mm(h2, wt, b_cols=tail["merge"], tn=1024, out_dtype=BF16, epi=_sigmoid, name="in_proj_merge",
                 **nt)
    zglu = _mm(h2, wt, b_cols=tail["glu"], tn=1024, out_dtype=F32, name="in_proj_glu", **nt)

    gn, gr, gs = w["qg"]
    q = _qprep(zq.reshape(bsz, s, ql), p["q_a_norm"], w["w_uq"], gn, gr, gs, cos, sin, n_heads)
    sources = ([kv_ctx] if kv_ctx is not None else []) + [kv_own]
    y_m = _attention(q, sources, gate_m.reshape(bsz, s, -1), 0, n_heads)
    y_f = _fourier_branch(zu.reshape(bsz, s, f), w["w_fnet"], gate_f.reshape(bsz, s, f), 0)
    y_c = _conv_branch(zglu.reshape(bsz, s, 2 * c), p["conv_w"], p["conv_b"], p["cln_g"], p["cln_b"],
                       w["w_pw2"], gate_c.reshape(bsz, s, c), 0)
    merged = _merge(y_f.reshape(m, f), y_m.reshape(m, -1), y_c.reshape(m, c),
                    w["w_br_f"], w["w_br_m"], w["w_br_c"], gmerge)
    out = _mm(merged, w["w_out"], out_dtype=F32, rows_per_batch=s,
              epi=lambda acc, xv, gv: xv + gv * acc,
              extras=(("mn", x.reshape(m, d)), ("bn", gate)), name="out_proj")
    return out.reshape(bsz, s, d)


def kernel(x, c, ctx, c_ctx, norm_g, w_ada, b_ada, w_in, q_a_norm, w_uq, kv_a_norm, w_ukv, q_norm, k_norm,
           w_fnet, conv_w, conv_b, cln_g, cln_b, w_pw2, w_br_f, w_br_m, w_br_c, w_out):
    bsz, n_tok, d = x.shape
    n_ctx = ctx.shape[1]
    depth = w_in.shape[0]
    f = w_fnet.shape[1]
    ql = q_a_norm.shape[1]
    kvl = kv_a_norm.shape[1]
    n_heads = w_uq.shape[2] // QK_DIM
    cdim = conv_b.shape[1]
    dims = (d, f, ql, kvl, n_heads, cdim)

    cos_l, sin_l = _rope_tables(n_tok)
    cos_c = jnp.ones((n_ctx, LANE), F32)
    sin_c = jnp.zeros((n_ctx, LANE), F32)

    n_rows = -(-(bsz + 1) // 8) * 8
    cond = jnp.concatenate([c, c_ctx[None, :], jnp.zeros((n_rows - bsz - 1, d), F32)], axis=0)

    w_in_t = jnp.swapaxes(w_in, 1, 2)

    xl, xc = x, ctx
    for l in range(depth):
        last = l == depth - 1
        w = _prep_layer(l, dims, w_in_t, w_uq, w_ukv, q_norm, k_norm, w_fnet, w_pw2, w_br_f, w_br_m, w_br_c, w_out)
        p = dict(q_a_norm=q_a_norm[l][None], conv_w=conv_w[l], conv_b=conv_b[l][None],
                 cln_g=cln_g[l][None], cln_b=cln_b[l][None])
        mod = _mm(cond, w_ada, b_layer=l, out_dtype=F32, tm=n_rows, tn=1024, tk=2048, a_act=_silu,
                  epi=lambda acc, bias: acc + bias, extras=(("n", b_ada[l][None]),), name="adaln")
        shift_l, scale_l, gate_l = (mod[:bsz, i * d:(i + 1) * d][:, None, :] for i in range(3))
        shift_c, scale_c, gate_c = (jnp.broadcast_to(mod[bsz, i * d:(i + 1) * d][None, None, :], (bsz, 1, d))
                                    for i in range(3))
        g = norm_g[l][None]
        hl = _norm_mod(xl, g, scale_l, shift_l)
        hc = _norm_mod(xc, g, scale_c, shift_c)
        kv_c = _kv_stream(hc.reshape(bsz * n_ctx, d), bsz, n_ctx, w, kv_a_norm[l][None], cos_c, sin_c, n_heads, kvl)
        kv_l = _kv_stream(hl.reshape(bsz * n_tok, d), bsz, n_tok, w, kv_a_norm[l][None], cos_l, sin_l, n_heads, kvl)
        new_xl = _full_stream(xl, hl, kv_l, kv_c, gate_l, w, p, cos_l, sin_l, dims)
        if not last:
            xc = _full_stream(xc, hc, kv_c, None, gate_c, w, p, cos_c, sin_c, dims)
        xl = new_xl
    return xl
```

```python
import functools
import math

import numpy as np
import jax
import jax.numpy as jnp
from jax import lax
from jax.experimental import pallas as pl
from jax.experimental.pallas import tpu as pltpu

F32 = jnp.float32
BF16 = jnp.bfloat16

EPS = 1e-6
QK_NOPE = 128
QK_ROPE = 64
V_DIM = 128
QK_DIM = QK_NOPE + QK_ROPE
HEAD_PAD = 256
LANE = 128
SUBLANE = 8
GRID_W = 64
CONV_W = 31
CONV_HALO = 16
ROPE_BASE = 10000.0
N_BRANCH = 3
ATTN_TQ = 1024
ATTN_TK = 2048
VMEM_LIMIT = 52 * 1024 * 1024


def _cparams(sem):
    return pltpu.CompilerParams(dimension_semantics=sem, vmem_limit_bytes=VMEM_LIMIT)


def _tile(n, pref, align=8):
    if n <= pref:
        return n
    t = (pref // align) * align
    while t >= align:
        if n % t == 0:
            return t
        t -= align
    return n


def _sigmoid(v):
    return 0.5 * jnp.tanh(0.5 * v) + 0.5


def _silu(v):
    return v * _sigmoid(v)


def _dot(a, b):
    return jnp.dot(a, b, preferred_element_type=F32)


MM_SUB = 256


def _mm_body(*refs, nk, n_extra, epi, a_act, b_t):
    a_ref, b_ref = refs[0], refs[1]
    extra = refs[2:2 + n_extra]
    o_ref = refs[2 + n_extra]
    a = a_ref[...]
    if a_act is not None:
        a = a_act(a.astype(F32))
    a = a.astype(BF16)
    tn = o_ref.shape[-1]

    def prod(c0, width):
        if b_t:
            return lax.dot_general(a, b_ref[c0:c0 + width, :].astype(BF16), (((1,), (1,)), ((), ())),
                                   preferred_element_type=F32)
        return _dot(a, b_ref[:, c0:c0 + width].astype(BF16))

    if nk == 1:
        sub = MM_SUB if tn % MM_SUB == 0 else tn
        for c0 in range(0, tn, sub):
            p = prod(c0, sub)
            o_ref[:, c0:c0 + sub] = epi(p, *[e[:, c0:c0 + sub] for e in extra]).astype(o_ref.dtype)
    else:
        p = prod(0, tn)
        acc_ref = refs[3 + n_extra]
        k = pl.program_id(2)

        @pl.when(k == 0)
        def _():
            acc_ref[...] = p

        @pl.when(k > 0)
        def _():
            acc_ref[...] += p

        @pl.when(k == nk - 1)
        def _():
            o_ref[...] = epi(acc_ref[...], *[e[...] for e in extra]).astype(o_ref.dtype)


def _mm(a, b, *, out_dtype, tm=1024, tn=512, tk=4096, epi=None, extras=(), a_act=None,
        rows_per_batch=None, b_cols=None, b_layer=None, b_t=False, name="mm"):
    m, kdim = a.shape
    col0, n = (0, b.shape[-2 if b_t else -1]) if b_cols is None else b_cols
    tm = _tile(m, tm)
    if rows_per_batch is not None:
        tm = _tile(rows_per_batch, tm)
    tn = _tile(n, tn, LANE)
    while col0 % tn:
        tn = _tile(n, tn - LANE, LANE)
    tk = _tile(kdim, tk, LANE)
    nk = kdim // tk
    jb = col0 // tn
    if epi is None:
        epi = lambda p: p
    lead = (None,) if b.ndim == 3 else ()
    lidx = (b_layer,) if b.ndim == 3 else ()
    if b_t:
        b_spec = pl.BlockSpec(lead + (tn, tk), lambda i, j, k: lidx + (j + jb, k))
    else:
        b_spec = pl.BlockSpec(lead + (tk, tn), lambda i, j, k: lidx + (k, j + jb))
    in_specs = [pl.BlockSpec((tm, tk), lambda i, j, k: (i, k)), b_spec]
    args = [a, b]
    for kind, arr in extras:
        if kind == "mn":
            in_specs.append(pl.BlockSpec((tm, tn), lambda i, j, k: (i, j)))
        elif kind == "n":
            in_specs.append(pl.BlockSpec((1, tn), lambda i, j, k: (0, j)))
        else:
            bpt = rows_per_batch // tm
            in_specs.append(pl.BlockSpec((None, 1, tn), lambda i, j, k, bpt=bpt: (i // bpt, 0, j)))
        args.append(arr)
    scratch = [pltpu.VMEM((tm, tn), F32)] if nk > 1 else []
    return pl.pallas_call(
        functools.partial(_mm_body, nk=nk, n_extra=len(extras), epi=epi, a_act=a_act, b_t=b_t),
        grid=(m // tm, n // tn, nk),
        in_specs=in_specs,
        out_specs=pl.BlockSpec((tm, tn), lambda i, j, k: (i, j)),
        out_shape=jax.ShapeDtypeStruct((m, n), out_dtype),
        scratch_shapes=scratch,
        compiler_params=_cparams(("parallel", "parallel", "arbitrary")),
        name=name,
    )(*args)


def _cast_rows_body(w_ref, o_ref):
    o_ref[...] = w_ref[...].astype(o_ref.dtype)


def _cast_rows_offset_body(main_ref, next_ref, o_ref):
    half = next_ref.shape[0]
    rb = main_ref.shape[0]
    o_ref[0:rb - half, :] = main_ref[half:rb, :].astype(o_ref.dtype)
    o_ref[rb - half:rb, :] = next_ref[...].astype(o_ref.dtype)


def _cast_rows(w, l, start, n_rows):
    kdim = w.shape[2]
    half = LANE // 2
    if start % LANE == half and n_rows % LANE == 0:
        base = start - half
        rb = 2 * LANE if base % (2 * LANE) == 0 and n_rows % (2 * LANE) == 0 else LANE
        jb, r = base // rb, rb // half
        return pl.pallas_call(
            _cast_rows_offset_body,
            grid=(n_rows // rb,),
            in_specs=[pl.BlockSpec((None, rb, kdim), lambda j: (l, jb + j, 0)),
                      pl.BlockSpec((None, half, kdim), lambda j: (l, (jb + j + 1) * r, 0))],
            out_specs=pl.BlockSpec((rb, kdim), lambda j: (j, 0)),
            out_shape=jax.ShapeDtypeStruct((n_rows, kdim), BF16),
            compiler_params=_cparams(("parallel",)),
            name="cast_rows_offset",
        )(w, w)
    rb = 512
    while start % rb or n_rows % rb:
        rb //= 2
    jb = start // rb
    return pl.pallas_call(
        _cast_rows_body,
        grid=(n_rows // rb,),
        in_specs=[pl.BlockSpec((None, rb, kdim), lambda j: (l, jb + j, 0))],
        out_specs=pl.BlockSpec((rb, kdim), lambda j: (j, 0)),
        out_shape=jax.ShapeDtypeStruct((n_rows, kdim), BF16),
        compiler_params=_cparams(("parallel",)),
        name="cast_rows",
    )(w)


def _norm_body(x_ref, g_ref, scale_ref, shift_ref, o_ref):
    x = x_ref[...]
    r = lax.rsqrt(jnp.mean(x * x, axis=-1, keepdims=True) + EPS)
    y = x * r * g_ref[...]
    o_ref[...] = (y * (1.0 + scale_ref[...]) + shift_ref[...]).astype(o_ref.dtype)


def _norm_mod(x, g, scale, shift):
    b, s, d = x.shape
    tr = _tile(s, 256)
    return pl.pallas_call(
        _norm_body,
        grid=(b, s // tr),
        in_specs=[pl.BlockSpec((None, tr, d), lambda bi, i: (bi, i, 0)),
                  pl.BlockSpec((1, d), lambda bi, i: (0, 0)),
                  pl.BlockSpec((None, 1, d), lambda bi, i: (bi, 0, 0)),
                  pl.BlockSpec((None, 1, d), lambda bi, i: (bi, 0, 0))],
        out_specs=pl.BlockSpec((None, tr, d), lambda bi, i: (bi, i, 0)),
        out_shape=jax.ShapeDtypeStruct((b, s, d), BF16),
        compiler_params=_cparams(("parallel", "parallel")),
        name="norm_mod",
    )(x, g, scale, shift)


def _swap_quarters(v, even_quarter):
    quarter = QK_ROPE // 4
    return jnp.where(even_quarter, pltpu.roll(v, LANE - quarter, axis=1), pltpu.roll(v, quarter, axis=1))


def _qprep_body(z_ref, an_ref, w_ref, gn_ref, gr_ref, gs_ref, cos_ref, sin_ref, q_ref, *, n_heads, scale):
    x = z_ref[...]
    r = lax.rsqrt(jnp.mean(x * x, axis=-1, keepdims=True) + EPS)
    xn = (x * r * an_ref[...]).astype(BF16)
    acc = _dot(xn, w_ref[...])
    cos = cos_ref[...] * gr_ref[...]
    sin = sin_ref[...] * gs_ref[...]
    gn = gn_ref[...]
    hw = n_heads * LANE
    for h in range(n_heads):
        nope = acc[:, h * LANE:(h + 1) * LANE]
        rp = acc[:, hw + h * LANE: hw + (h + 1) * LANE]
        rs = acc[:, 2 * hw + h * LANE: 2 * hw + (h + 1) * LANE]
        ss = jnp.sum(nope * nope, axis=-1, keepdims=True) + jnp.sum(rp * rp, axis=-1, keepdims=True)
        inv = lax.rsqrt(ss * (1.0 / QK_DIM) + EPS) * scale
        q_ref[h, :, 0:LANE] = (nope * inv * gn).astype(q_ref.dtype)
        q_ref[h, :, LANE:HEAD_PAD] = ((rp * cos + rs * sin) * inv).astype(q_ref.dtype)


def _qprep(zq, a_norm, w_p, gn, gr, gs, cos, sin, n_heads):
    b, s, ql = zq.shape
    tr = _tile(s, 256)
    nw = w_p.shape[1]
    return pl.pallas_call(
        functools.partial(_qprep_body, n_heads=n_heads, scale=QK_DIM ** -0.5),
        grid=(b, s // tr),
        in_specs=[pl.BlockSpec((None, tr, ql), lambda bi, i: (bi, i, 0)),
                  pl.BlockSpec((1, ql), lambda bi, i: (0, 0)),
                  pl.BlockSpec((ql, nw), lambda bi, i: (0, 0)),
                  pl.BlockSpec((1, LANE), lambda bi, i: (0, 0)),
                  pl.BlockSpec((1, LANE), lambda bi, i: (0, 0)),
                  pl.BlockSpec((1, LANE), lambda bi, i: (0, 0)),
                  pl.BlockSpec((tr, LANE), lambda bi, i: (i, 0)),
                  pl.BlockSpec((tr, LANE), lambda bi, i: (i, 0))],
        out_specs=pl.BlockSpec((None, n_heads, tr, HEAD_PAD), lambda bi, i: (bi, 0, i, 0)),
        out_shape=jax.ShapeDtypeStruct((b, n_heads, s, HEAD_PAD), BF16),
        compiler_params=_cparams(("parallel", "parallel")),
        name="q_prep",
    )(zq, a_norm, w_p, gn, gr, gs, cos, sin)


def _kvprep_body(z_ref, an_ref, w_ref, gn_ref, gr_ref, gs_ref, cos_ref, sin_ref, k_ref, v_ref, *,
                 n_heads, kv_lora):
    x = z_ref[:, 0:kv_lora]
    tr = x.shape[0]
    lane = lax.broadcasted_iota(jnp.int32, (tr, LANE), 1)
    kr = jnp.where(lane < QK_ROPE, z_ref[:, kv_lora:kv_lora + LANE], 0.0)
    krs = _swap_quarters(kr, (lane // (QK_ROPE // 4)) % 2 == 0)
    r = lax.rsqrt(jnp.mean(x * x, axis=-1, keepdims=True) + EPS)
    xn = (x * r * an_ref[...]).astype(BF16)
    acc = _dot(xn, w_ref[...])
    ss_r = jnp.sum(kr * kr, axis=-1, keepdims=True)
    rope = kr * (cos_ref[...] * gr_ref[...]) + krs * (sin_ref[...] * gs_ref[...])
    gn = gn_ref[...]
    hw = n_heads * LANE
    ones_col = (lane == 0).astype(v_ref.dtype)
    for h in range(n_heads):
        kn = acc[:, h * LANE:(h + 1) * LANE]
        ss = jnp.sum(kn * kn, axis=-1, keepdims=True) + ss_r
        inv = lax.rsqrt(ss * (1.0 / QK_DIM) + EPS)
        k_ref[h, :, 0:LANE] = (kn * inv * gn).astype(k_ref.dtype)
        k_ref[h, :, LANE:HEAD_PAD] = (rope * inv).astype(k_ref.dtype)
        v_ref[h, :, 0:LANE] = acc[:, hw + h * LANE: hw + (h + 1) * LANE].astype(v_ref.dtype)
        v_ref[h, :, LANE:HEAD_PAD] = ones_col


def _kvprep(zkv, a_norm, w_p, gn, gr, gs, cos, sin, n_heads, kv_lora):
    b, s, zw = zkv.shape
    tr = _tile(s, 256)
    nw = w_p.shape[1]
    out = jax.ShapeDtypeStruct((b, n_heads, s, HEAD_PAD), BF16)
    ospec = pl.BlockSpec((None, n_heads, tr, HEAD_PAD), lambda bi, i: (bi, 0, i, 0))
    return pl.pallas_call(
        functools.partial(_kvprep_body, n_heads=n_heads, kv_lora=kv_lora),
        grid=(b, s // tr),
        in_specs=[pl.BlockSpec((None, tr, zw), lambda bi, i: (bi, i, 0)),
                  pl.BlockSpec((1, kv_lora), lambda bi, i: (0, 0)),
                  pl.BlockSpec((kv_lora, nw), lambda bi, i: (0, 0)),
                  pl.BlockSpec((1, LANE), lambda bi, i: (0, 0)),
                  pl.BlockSpec((1, LANE), lambda bi, i: (0, 0)),
                  pl.BlockSpec((1, LANE), lambda bi, i: (0, 0)),
                  pl.BlockSpec((tr, LANE), lambda bi, i: (i, 0)),
                  pl.BlockSpec((tr, LANE), lambda bi, i: (i, 0))],
        out_specs=[ospec, ospec],
        out_shape=[out, out],
        compiler_params=_cparams(("parallel", "parallel")),
        name="kv_prep",
    )(zkv, a_norm, w_p, gn, gr, gs, cos, sin)


def _attn_body(*refs, n_src, chunks):
    q_ref = refs[0]
    kv = refs[1:1 + 2 * n_src]
    gate_ref = refs[1 + 2 * n_src]
    o_ref = refs[2 + 2 * n_src]
    q = q_ref[...]
    tq = q.shape[0]
    plan = []
    for si in range(n_src):
        tk, n_chunks = chunks[si]
        plan += [(kv[2 * si], kv[2 * si + 1], ci * tk, tk) for ci in range(n_chunks)]

    def scores(i):
        k_ref, _, start, tk = plan[i]
        return lax.dot_general(q, k_ref[start:start + tk, :], (((1,), (1,)), ((), ())),
                               preferred_element_type=F32)

    m = jnp.full((tq, LANE), -jnp.inf, F32)
    acc = jnp.zeros((tq, HEAD_PAD), F32)
    s_next = scores(0)
    for i in range(len(plan)):
        s = s_next
        if i + 1 < len(plan):
            s_next = scores(i + 1)
        _, v_ref, start, tk = plan[i]
        m_new = jnp.maximum(m, jnp.max(s, axis=-1, keepdims=True))
        p = jnp.exp((s - jnp.tile(m_new, (1, tk // LANE))).astype(BF16))
        alpha = jnp.exp(m - m_new)
        acc = acc * jnp.tile(alpha, (1, HEAD_PAD // LANE)) + _dot(p, v_ref[start:start + tk, :])
        m = m_new
    out = acc[:, 0:V_DIM] / acc[:, V_DIM:V_DIM + 1]
    o_ref[...] = (out * gate_ref[...].astype(F32)).astype(o_ref.dtype)


def _attention(q, sources, gates, gate_col, n_heads):
    b, _, s, _ = q.shape
    tq = _tile(s, ATTN_TQ)
    in_specs = [pl.BlockSpec((None, None, tq, HEAD_PAD), lambda bi, h, i: (bi, h, i, 0))]
    args = [q]
    chunks = []
    for k, v in sources:
        sk = k.shape[2]
        tk = _tile(sk, ATTN_TK)
        chunks.append((tk, sk // tk))
        spec = pl.BlockSpec((None, None, sk, HEAD_PAD), lambda bi, h, i: (bi, h, 0, 0))
        in_specs += [spec, spec]
        args += [k, v]
    gb = gate_col // V_DIM
    in_specs.append(pl.BlockSpec((None, tq, V_DIM), lambda bi, h, i: (bi, i, gb + h)))
    args.append(gates)
    return pl.pallas_call(
        functools.partial(_attn_body, n_src=len(sources), chunks=tuple(chunks)),
        grid=(b, n_heads, s // tq),
        in_specs=in_specs,
        out_specs=pl.BlockSpec((None, tq, V_DIM), lambda bi, h, i: (bi, i, h)),
        out_shape=jax.ShapeDtypeStruct((b, s, n_heads * V_DIM), BF16),
        compiler_params=_cparams(("parallel", "parallel", "arbitrary")),
        name="attention",
    )(*args)


def _fft_factors(s):
    n1 = 128 if s % 128 == 0 and s // 128 >= 16 else 16
    return n1, s // n1


@functools.lru_cache(maxsize=None)
def _fft_tables(s, c):
    n1, n2 = _fft_factors(s)
    k1 = np.arange(n1, dtype=np.float64)
    a1 = 2.0 * np.pi * np.outer(k1, k1) / n1
    t1 = np.concatenate([np.cos(a1), -np.sin(a1)], axis=0)
    at = 2.0 * np.pi * np.outer(np.arange(n2, dtype=np.float64), k1) / s
    twr = np.cos(at)[:, :, None]
    twi = (-np.sin(at))[:, :, None]
    k2 = np.arange(n2, dtype=np.float64)
    a3 = 2.0 * np.pi * np.outer(k2, k2) / n2
    c3, s3 = np.cos(a3), np.sin(a3)
    t3 = np.block([[c3, s3], [-s3, c3]])
    kc = np.arange(c, dtype=np.float64)
    ac = 2.0 * np.pi * np.outer(kc, kc) / c
    tc = np.stack([np.cos(ac), np.sin(ac)], axis=0) / math.sqrt(float(s) * float(c))

    bf = lambda t: t.astype(np.float32).astype(BF16)
    return dict(t1=bf(t1), twr=twr.astype(np.float32), twi=twi.astype(np.float32), t3=bf(t3), tc=bf(tc))


def _fft1_body(u_ref, t_ref, twr_ref, twi_ref, y_ref, *, n1):
    y = _dot(t_ref[...], u_ref[...])
    yr, yi = y[0:n1], y[n1:2 * n1]
    tr, ti = twr_ref[...], twi_ref[...]
    y_ref[0] = (yr * tr - yi * ti).astype(y_ref.dtype)
    y_ref[1] = (yr * ti + yi * tr).astype(y_ref.dtype)


def _fft3_body(y_ref, t_ref, x_ref):
    x_ref[...] = _dot(t_ref[...], y_ref[...]).astype(x_ref.dtype)


def _fftc_body(x_ref, t_ref, wf_ref, gate_ref, o_ref):
    z = _dot(x_ref[0], t_ref[0]) + _dot(x_ref[1], t_ref[1])
    y = _dot(z.astype(BF16), wf_ref[...])
    o_ref[...] = (y * gate_ref[...].astype(F32)).astype(o_ref.dtype)


def _fourier_branch(u, w_fnet, gates, gate_col):
    b, s, c = u.shape
    n1, n2 = _fft_factors(s)
    tabs = _fft_tables(s, c)
    t1, t3, tc = (jnp.asarray(tabs[k]) for k in ("t1", "t3", "tc"))
    twr, twi = jnp.asarray(tabs["twr"]), jnp.asarray(tabs["twi"])

    const2 = lambda bi, j: (0, 0)
    y = pl.pallas_call(
        functools.partial(_fft1_body, n1=n1),
        grid=(b, n2),
        in_specs=[pl.BlockSpec((None, n1, c), lambda bi, j: (bi, 0, j)),
                  pl.BlockSpec((2 * n1, n1), const2),
                  pl.BlockSpec((None, n1, 1), lambda bi, j: (j, 0, 0)),
                  pl.BlockSpec((None, n1, 1), lambda bi, j: (j, 0, 0))],
        out_specs=pl.BlockSpec((None, 2, None, n1, c), lambda bi, j: (bi, 0, j, 0, 0)),
        out_shape=jax.ShapeDtypeStruct((b, 2, n2, n1, c), BF16),
        compiler_params=_cparams(("parallel", "parallel")),
        name="fft_stage1",
    )(u.reshape(b, n1, n2 * c), t1, twr, twi)

    cols = n1 * c
    tn = _tile(cols, 8192, LANE)
    x = pl.pallas_call(
        _fft3_body,
        grid=(b, cols // tn),
        in_specs=[pl.BlockSpec((None, 2 * n2, tn), lambda bi, j: (bi, 0, j)),
                  pl.BlockSpec((2 * n2, 2 * n2), const2)],
        out_specs=pl.BlockSpec((None, 2 * n2, tn), lambda bi, j: (bi, 0, j)),
        out_shape=jax.ShapeDtypeStruct((b, 2 * n2, cols), BF16),
        compiler_params=_cparams(("parallel", "parallel")),
        name="fft_stage2",
    )(y.reshape(b, 2 * n2, cols), t3)

    tm = _tile(s, 1024)
    gb = gate_col // c
    const3 = lambda bi, i: (0, 0, 0)
    return pl.pallas_call(
        _fftc_body,
        grid=(b, s // tm),
        in_specs=[pl.BlockSpec((None, 2, tm, c), lambda bi, i: (bi, 0, i, 0)),
                  pl.BlockSpec((2, c, c), const3),
                  pl.BlockSpec((c, c), lambda bi, i: (0, 0)),
                  pl.BlockSpec((None, tm, c), lambda bi, i: (bi, i, gb))],
        out_specs=pl.BlockSpec((None, tm, c), lambda bi, i: (bi, i, 0)),
        out_shape=jax.ShapeDtypeStruct((b, s, c), BF16),
        compiler_params=_cparams(("parallel", "parallel")),
        name="fft_channel",
    )(x.reshape(b, 2, s, c), tc, w_fnet, gates)


def _conv_body(zl_ref, zc_ref, zr_ref, cw_ref, cb_ref, lg_ref, lb_ref, w_ref, gate_ref, o_ref, ext_ref, sh_ref,
               *, c, ts, n_tiles):
    i = pl.program_id(1)

    def glu(z):
        return z[:, 0:c] * _sigmoid(z[:, c:2 * c])

    left = jnp.where(i > 0, glu(zl_ref[...]), 0.0)
    right = jnp.where(i < n_tiles - 1, glu(zr_ref[...]), 0.0)
    ext_ref[0:CONV_HALO, :] = left
    ext_ref[CONV_HALO:CONV_HALO + ts, :] = glu(zc_ref[...])
    ext_ref[CONV_HALO + ts:2 * CONV_HALO + ts, :] = right
    span = sh_ref.shape[1]
    for r in range(1, SUBLANE):
        sh_ref[r - 1] = ext_ref[r:r + span, :]
    off = CONV_HALO - CONV_W // 2
    acc = jnp.zeros((ts, c), F32) + cb_ref[...]
    for j in range(CONV_W):
        r, base = (off + j) % SUBLANE, (off + j) // SUBLANE * SUBLANE
        src = ext_ref[base:base + ts, :] if r == 0 else sh_ref[r - 1, base:base + ts, :]
        acc = acc + src * cw_ref[j:j + 1, :]
    mu = jnp.mean(acc, axis=-1, keepdims=True)
    d = acc - mu
    var = jnp.mean(d * d, axis=-1, keepdims=True)
    y = _silu(d * lax.rsqrt(var + EPS) * lg_ref[...] + lb_ref[...])
    out = _dot(y.astype(BF16), w_ref[...])
    o_ref[...] = (out * gate_ref[...].astype(F32)).astype(o_ref.dtype)


def _conv_branch(zglu, conv_w, conv_b, cln_g, cln_b, w_pw2, gates, gate_col):
    b, s, c2 = zglu.shape
    c = c2 // 2
    ts = _tile(s, 256, CONV_HALO)
    n_tiles = s // ts
    hb = ts // CONV_HALO
    n_hblk = s // CONV_HALO
    gb = gate_col // c
    row = lambda bi, i: (0, 0)
    return pl.pallas_call(
        functools.partial(_conv_body, c=c, ts=ts, n_tiles=n_tiles),
        grid=(b, n_tiles),
        in_specs=[pl.BlockSpec((None, CONV_HALO, c2), lambda bi, i: (bi, jnp.maximum(i * hb - 1, 0), 0)),
                  pl.BlockSpec((None, ts, c2), lambda bi, i: (bi, i, 0)),
                  pl.BlockSpec((None, CONV_HALO, c2),
                               lambda bi, i: (bi, jnp.minimum((i + 1) * hb, n_hblk - 1), 0)),
                  pl.BlockSpec((CONV_W, c), row),
                  pl.BlockSpec((1, c), row),
                  pl.BlockSpec((1, c), row),
                  pl.BlockSpec((1, c), row),
                  pl.BlockSpec((c, c), row),
                  pl.BlockSpec((None, ts, c), lambda bi, i: (bi, i, gb))],
        out_specs=pl.BlockSpec((None, ts, c), lambda bi, i: (bi, i, 0)),
        out_shape=jax.ShapeDtypeStruct((b, s, c), BF16),
        scratch_shapes=[pltpu.VMEM((ts + 2 * CONV_HALO, c), F32),
                        pltpu.VMEM((SUBLANE - 1, ts + 2 * CONV_HALO - SUBLANE, c), F32)],
        compiler_params=_cparams(("parallel", "arbitrary")),
        name="conv_branch",
    )(zglu, zglu, zglu, conv_w, conv_b, cln_g, cln_b, w_pw2, gates)


def _merge_body(yf_ref, ym_ref, yc_ref, wf_ref, wm_ref, wc_ref, gf_ref, gm_ref, gc_ref, o_ref):
    acc = gf_ref[...].astype(F32) * _dot(yf_ref[...], wf_ref[...])
    acc = acc + gm_ref[...].astype(F32) * _dot(ym_ref[...], wm_ref[...])
    acc = acc + gc_ref[...].astype(F32) * _dot(yc_ref[...], wc_ref[...])
    o_ref[...] = acc.astype(o_ref.dtype)


def _merge(yf, ym, yc, wf, wm, wc, g):
    m = yf.shape[0]
    d = wf.shape[1]
    tm = _tile(m, 512)
    tn = _tile(d, 1024, LANE)
    nj = d // tn
    yspec = lambda arr: pl.BlockSpec((tm, arr.shape[1]), lambda j, i: (i, 0))
    wspec = lambda arr: pl.BlockSpec((arr.shape[0], tn), lambda j, i: (0, j))
    gspec = lambda t: pl.BlockSpec((tm, tn), lambda j, i, t=t: (i, t * nj + j))
    return pl.pallas_call(
        _merge_body,
        grid=(nj, m // tm),
        in_specs=[yspec(yf), yspec(ym), yspec(yc), wspec(wf), wspec(wm), wspec(wc),
                  gspec(0), gspec(1), gspec(2)],
        out_specs=pl.BlockSpec((tm, tn), lambda j, i: (i, j)),
        out_shape=jax.ShapeDtypeStruct((m, d), BF16),
        compiler_params=_cparams(("parallel", "parallel")),
        name="merge",
    )(yf, ym, yc, wf, wm, wc, g, g, g)


def _rope_tables(n_tok):
    half = QK_ROPE // 4
    inv = ROPE_BASE ** (-jnp.arange(half, dtype=F32) / half)
    t = jnp.arange(n_tok, dtype=jnp.int32)
    rows = (t // GRID_W).astype(F32)[:, None] * inv[None, :]
    cols = (t % GRID_W).astype(F32)[:, None] * inv[None, :]
    cr, sr, cc, sc = jnp.cos(rows), jnp.sin(rows), jnp.cos(cols), jnp.sin(cols)
    pad1 = jnp.ones((n_tok, LANE - QK_ROPE), F32)
    pad0 = jnp.zeros((n_tok, LANE - QK_ROPE), F32)
    cos = jnp.concatenate([cr, cr, cc, cc, pad1], axis=1)
    sin = jnp.concatenate([-sr, sr, -sc, sc, pad0], axis=1)
    return cos, sin


def _swap_perm():
    q = QK_ROPE // 4
    return np.concatenate([np.arange(q, 2 * q), np.arange(0, q), np.arange(3 * q, 4 * q), np.arange(2 * q, 3 * q)])


def _pad_lanes(w, width=LANE):
    return jnp.pad(w, [(0, 0)] * (w.ndim - 1) + [(0, width - w.shape[-1])])


def _prep_layer(l, dims, w_in, w_uq, w_ukv, q_norm, k_norm, w_fnet, w_pw2, w_br_f, w_br_m, w_br_c, w_out):
    d, f, ql, kvl, h, c = dims
    mla = h * V_DIM
    off_fg = f
    off_q = 2 * f
    off_kv = off_q + ql
    off_kr = off_kv + kvl
    off_mg = off_kr + QK_ROPE
    off_glu = off_mg + mla
    off_cg = off_glu + 2 * c
    off_merge = off_cg + c
    perm = _swap_perm()
    w_u = _cast_rows(w_in, l, 0, f)
    w_fg = _cast_rows(w_in, l, off_fg, f)
    w_q = _cast_rows(w_in, l, off_q, ql)
    w_kv = _cast_rows(w_in, l, off_kv, kvl + LANE)
    n_in = w_in.shape[1]
    w_tail = _cast_rows(w_in, l, off_mg, n_in - off_mg)
    tail = dict(mg=(0, mla), glu=(off_glu - off_mg, 2 * c), cg=(off_cg - off_mg, c),
                merge=(off_merge - off_mg, n_in - off_merge))

    wq3 = w_uq[l].reshape(ql, h, QK_DIM)
    q_nope = wq3[:, :, :QK_NOPE].reshape(ql, h * LANE)
    q_rope = wq3[:, :, QK_NOPE:]
    w_uq_p = jnp.concatenate([q_nope, _pad_lanes(q_rope).reshape(ql, h * LANE),
                              _pad_lanes(q_rope[:, :, perm]).reshape(ql, h * LANE)], axis=1).astype(BF16)
    wkv3 = w_ukv[l].reshape(kvl, h, QK_NOPE + V_DIM)
    w_ukv_p = jnp.concatenate([wkv3[:, :, :QK_NOPE].reshape(kvl, h * LANE),
                               wkv3[:, :, QK_NOPE:].reshape(kvl, h * LANE)], axis=1).astype(BF16)

    def gains(g):
        return (g[None, :QK_NOPE], _pad_lanes(g[None, QK_NOPE:]), _pad_lanes(g[None, QK_NOPE:][:, perm]))

    return dict(w_u=w_u, w_fg=w_fg, w_q=w_q, w_kv=w_kv, w_tail=w_tail, tail=tail,
                w_uq=w_uq_p, w_ukv=w_ukv_p, qg=gains(q_norm[l]), kg=gains(k_norm[l]),
                w_fnet=w_fnet[l].astype(BF16), w_pw2=w_pw2[l].astype(BF16),
                w_br_f=w_br_f[l].astype(BF16), w_br_m=w_br_m[l].astype(BF16), w_br_c=w_br_c[l].astype(BF16),
                w_out=w_out[l].astype(BF16))


def _kv_stream(h2, bsz, s, w, kv_a_norm, cos, sin, n_heads, kv_lora):
    zkv = _mm(h2, w["w_kv"], b_t=True, out_dtype=F32, tn=kv_lora + LANE, name="in_proj_kv")
    gn, gr, gs = w["kg"]
    return _kvprep(zkv.reshape(bsz, s, -1), kv_a_norm, w["w_ukv"], gn, gr, gs, cos, sin, n_heads, kv_lora)


def _full_stream(x, h, kv_own, kv_ctx, gate, w, p, cos, sin, dims):
    d, f, ql, kvl, n_heads, c = dims
    bsz, s, _ = x.shape
    m = bsz * s
    h2 = h.reshape(m, d)
    wt, tail = w["w_tail"], w["tail"]
    nt = dict(b_t=True)
    zu = _mm(h2, w["w_u"], tn=1024, out_dtype=BF16, name="in_proj_u", **nt)
    gate_f = _mm(h2, w["w_fg"], tn=1024, out_dtype=BF16, epi=_silu, name="in_proj_gate_f", **nt)
    zq = _mm(h2, w["w_q"], tn=ql, out_dtype=F32, name="in_proj_q", **nt)
    gate_m = _mm(h2, wt, b_cols=tail["mg"], tn=1024, out_dtype=BF16, epi=_silu, name="in_proj_gate_m", **nt)
    gate_c = _mm(h2, wt, b_cols=tail["cg"], tn=1024, out_dtype=BF16, epi=_silu, name="in_proj_gate_c", **nt)
    gmerge = _mm(h2, wt, b_cols=tail["merge"], tn=1024, out_dtype=BF16, epi=_sigmoid, name="in_proj_merge",
                 **nt)
    zglu = _mm(h2, wt, b_cols=tail["glu"], tn=1024, out_dtype=F32, name="in_proj_glu", **nt)

    gn, gr, gs = w["qg"]
    q = _qprep(zq.reshape(bsz, s, ql), p["q_a_norm"], w["w_uq"], gn, gr, gs, cos, sin, n_heads)
    sources = ([kv_ctx] if kv_ctx is not None else []) + [kv_own]
    y_m = _attention(q, sources, gate_m.reshape(bsz, s, -1), 0, n_heads)
    y_f = _fourier_branch(zu.reshape(bsz, s, f), w["w_fnet"], gate_f.reshape(bsz, s, f), 0)
    y_c = _conv_branch(zglu.reshape(bsz, s, 2 * c), p["conv_w"], p["conv_b"], p["cln_g"], p["cln_b"],
                       w["w_pw2"], gate_c.reshape(bsz, s, c), 0)
    merged = _merge(y_f.reshape(m, f), y_m.reshape(m, -1), y_c.reshape(m, c),
                    w["w_br_f"], w["w_br_m"], w["w_br_c"], gmerge)
    out = _mm(merged, w["w_out"], out_dtype=F32, rows_per_batch=s,
              epi=lambda acc, xv, gv: xv + gv * acc,
              extras=(("mn", x.reshape(m, d)), ("bn", gate)), name="out_proj")
    return out.reshape(bsz, s, d)


def kernel(x, c, ctx, c_ctx, norm_g, w_ada, b_ada, w_in, q_a_norm, w_uq, kv_a_norm, w_ukv, q_norm, k_norm,
           w_fnet, conv_w, conv_b, cln_g, cln_b, w_pw2, w_br_f, w_br_m, w_br_c, w_out):
    bsz, n_tok, d = x.shape
    n_ctx = ctx.shape[1]
    depth = w_in.shape[0]
    f = w_fnet.shape[1]
    ql = q_a_norm.shape[1]
    kvl = kv_a_norm.shape[1]
    n_heads = w_uq.shape[2] // QK_DIM
    cdim = conv_b.shape[1]
    dims = (d, f, ql, kvl, n_heads, cdim)

    cos_l, sin_l = _rope_tables(n_tok)
    cos_c = jnp.ones((n_ctx, LANE), F32)
    sin_c = jnp.zeros((n_ctx, LANE), F32)

    n_rows = -(-(bsz + 1) // 8) * 8
    cond = jnp.concatenate([c, c_ctx[None, :], jnp.zeros((n_rows - bsz - 1, d), F32)], axis=0)

    w_in_t = jnp.swapaxes(w_in, 1, 2)

    xl, xc = x, ctx
    for l in range(depth):
        last = l == depth - 1
        w = _prep_layer(l, dims, w_in_t, w_uq, w_ukv, q_norm, k_norm, w_fnet, w_pw2, w_br_f, w_br_m, w_br_c, w_out)
        p = dict(q_a_norm=q_a_norm[l][None], conv_w=conv_w[l], conv_b=conv_b[l][None],
                 cln_g=cln_g[l][None], cln_b=cln_b[l][None])
        mod = _mm(cond, w_ada, b_layer=l, out_dtype=F32, tm=n_rows, tn=1024, tk=2048, a_act=_silu,
                  epi=lambda acc, bias: acc + bias, extras=(("n", b_ada[l][None]),), name="adaln")
        shift_l, scale_l, gate_l = (mod[:bsz, i * d:(i + 1) * d][:, None, :] for i in range(3))
        shift_c, scale_c, gate_c = (jnp.broadcast_to(mod[bsz, i * d:(i + 1) * d][None, None, :], (bsz, 1, d))
                                    for i in range(3))
        g = norm_g[l][None]
        hl = _norm_mod(xl, g, scale_l, shift_l)
        hc = _norm_mod(xc, g, scale_c, shift_c)
        kv_c = _kv_stream(hc.reshape(bsz * n_ctx, d), bsz, n_ctx, w, kv_a_norm[l][None], cos_c, sin_c, n_heads, kvl)
        kv_l = _kv_stream(hl.reshape(bsz * n_tok, d), bsz, n_tok, w, kv_a_norm[l][None], cos_l, sin_l, n_heads, kvl)
        new_xl = _full_stream(xl, hl, kv_l, kv_c, gate_l, w, p, cos_l, sin_l, dims)
        if not last:
            xc = _full_stream(xc, hc, kv_c, None, gate_c, w, p, cos_c, sin_c, dims)
        xl = new_xl
    return xl
```

```python
import functools
import math

import numpy as np
import jax
import jax.numpy as jnp
from jax import lax
from jax.experimental import pallas as pl
from jax.experimental.pallas import tpu as pltpu

F32 = jnp.float32
BF16 = jnp.bfloat16

EPS = 1e-6
QK_NOPE = 128
QK_ROPE = 64
V_DIM = 128
QK_DIM = QK_NOPE + QK_ROPE
HEAD_PAD = 256
LANE = 128
SUBLANE = 8
GRID_W = 64
CONV_W = 31
CONV_HALO = 16
ROPE_BASE = 10000.0
N_BRANCH = 3
ATTN_TQ = 1024
ATTN_TK = 2048
VMEM_LIMIT = 52 * 1024 * 1024


def _cparams(sem):
    return pltpu.CompilerParams(dimension_semantics=sem, vmem_limit_bytes=VMEM_LIMIT)


def _tile(n, pref, align=8):
    if n <= pref:
        return n
    t = (pref // align) * align
    while t >= align:
        if n % t == 0:
            return t
        t -= align
    return n


def _sigmoid(v):
    return 0.5 * jnp.tanh(0.5 * v) + 0.5


def _silu(v):
    return v * _sigmoid(v)


def _dot(a, b):
    return jnp.dot(a, b, preferred_element_type=F32)


MM_SUB = 256


def _mm_body(*refs, nk, n_extra, epi, a_act, b_t):
    a_ref, b_ref = refs[0], refs[1]
    extra = refs[2:2 + n_extra]
    o_ref = refs[2 + n_extra]
    a = a_ref[...]
    if a_act is not None:
        a = a_act(a.astype(F32))
    a = a.astype(BF16)
    tn = o_ref.shape[-1]

    def prod(c0, width):
        if b_t:
            return lax.dot_general(a, b_ref[c0:c0 + width, :].astype(BF16), (((1,), (1,)), ((), ())),
                                   preferred_element_type=F32)
        return _dot(a, b_ref[:, c0:c0 + width].astype(BF16))

    if nk == 1:
        sub = MM_SUB if tn % MM_SUB == 0 else tn
        for c0 in range(0, tn, sub):
            p = prod(c0, sub)
            o_ref[:, c0:c0 + sub] = epi(p, *[e[:, c0:c0 + sub] for e in extra]).astype(o_ref.dtype)
    else:
        p = prod(0, tn)
        acc_ref = refs[3 + n_extra]
        k = pl.program_id(2)

        @pl.when(k == 0)
        def _():
            acc_ref[...] = p

        @pl.when(k > 0)
        def _():
            acc_ref[...] += p

        @pl.when(k == nk - 1)
        def _():
            o_ref[...] = epi(acc_ref[...], *[e[...] for e in extra]).astype(o_ref.dtype)


def _mm(a, b, *, out_dtype, tm=1024, tn=512, tk=4096, epi=None, extras=(), a_act=None,
        rows_per_batch=None, b_cols=None, b_layer=None, b_t=False, name="mm"):
    m, kdim = a.shape
    col0, n = (0, b.shape[-2 if b_t else -1]) if b_cols is None else b_cols
    tm = _tile(m, tm)
    if rows_per_batch is not None:
        tm = _tile(rows_per_batch, tm)
    tn = _tile(n, tn, LANE)
    while col0 % tn:
        tn = _tile(n, tn - LANE, LANE)
    tk = _tile(kdim, tk, LANE)
    nk = kdim // tk
    jb = col0 // tn
    if epi is None:
        epi = lambda p: p
    lead = (None,) if b.ndim == 3 else ()
    lidx = (b_layer,) if b.ndim == 3 else ()
    if b_t:
        b_spec = pl.BlockSpec(lead + (tn, tk), lambda i, j, k: lidx + (j + jb, k))
    else:
        b_spec = pl.BlockSpec(lead + (tk, tn), lambda i, j, k: lidx + (k, j + jb))
    in_specs = [pl.BlockSpec((tm, tk), lambda i, j, k: (i, k)), b_spec]
    args = [a, b]
    for kind, arr in extras:
        if kind == "mn":
            in_specs.append(pl.BlockSpec((tm, tn), lambda i, j, k: (i, j)))
        elif kind == "n":
            in_specs.append(pl.BlockSpec((1, tn), lambda i, j, k: (0, j)))
        else:
            bpt = rows_per_batch // tm
            in_specs.append(pl.BlockSpec((None, 1, tn), lambda i, j, k, bpt=bpt: (i // bpt, 0, j)))
        args.append(arr)
    scratch = [pltpu.VMEM((tm, tn), F32)] if nk > 1 else []
    return pl.pallas_call(
        functools.partial(_mm_body, nk=nk, n_extra=len(extras), epi=epi, a_act=a_act, b_t=b_t),
        grid=(m // tm, n // tn, nk),
        in_specs=in_specs,
        out_specs=pl.BlockSpec((tm, tn), lambda i, j, k: (i, j)),
        out_shape=jax.ShapeDtypeStruct((m, n), out_dtype),
        scratch_shapes=scratch,
        compiler_params=_cparams(("parallel", "parallel", "arbitrary")),
        name=name,
    )(*args)


def _cast_rows_body(w_ref, o_ref):
    o_ref[...] = w_ref[...].astype(o_ref.dtype)


def _cast_rows_offset_body(main_ref, next_ref, o_ref):
    half = next_ref.shape[0]
    rb = main_ref.shape[0]
    o_ref[0:rb - half, :] = main_ref[half:rb, :].astype(o_ref.dtype)
    o_ref[rb - half:rb, :] = next_ref[...].astype(o_ref.dtype)


def _cast_rows(w, l, start, n_rows):
    kdim = w.shape[2]
    half = LANE // 2
    if start % LANE == half and n_rows % LANE == 0:
        base = start - half
        rb = 2 * LANE if base % (2 * LANE) == 0 and n_rows % (2 * LANE) == 0 else LANE
        jb, r = base // rb, rb // half
        return pl.pallas_call(
            _cast_rows_offset_body,
            grid=(n_rows // rb,),
            in_specs=[pl.BlockSpec((None, rb, kdim), lambda j: (l, jb + j, 0)),
                      pl.BlockSpec((None, half, kdim), lambda j: (l, (jb + j + 1) * r, 0))],
            out_specs=pl.BlockSpec((rb, kdim), lambda j: (j, 0)),
            out_shape=jax.ShapeDtypeStruct((n_rows, kdim), BF16),
            compiler_params=_cparams(("parallel",)),
            name="cast_rows_offset",
        )(w, w)
    rb = 512
    while start % rb or n_rows % rb:
        rb //= 2
    jb = start // rb
    return pl.pallas_call(
        _cast_rows_body,
        grid=(n_rows // rb,),
        in_specs=[pl.BlockSpec((None, rb, kdim), lambda j: (l, jb + j, 0))],
        out_specs=pl.BlockSpec((rb, kdim), lambda j: (j, 0)),
        out_shape=jax.ShapeDtypeStruct((n_rows, kdim), BF16),
        compiler_params=_cparams(("parallel",)),
        name="cast_rows",
    )(w)


def _norm_body(x_ref, g_ref, scale_ref, shift_ref, o_ref):
    x = x_ref[...]
    r = lax.rsqrt(jnp.mean(x * x, axis=-1, keepdims=True) + EPS)
    y = x * r * g_ref[...]
    o_ref[...] = (y * (1.0 + scale_ref[...]) + shift_ref[...]).astype(o_ref.dtype)


def _norm_mod(x, g, scale, shift):
    b, s, d = x.shape
    tr = _tile(s, 256)
    return pl.pallas_call(
        _norm_body,
        grid=(b, s // tr),
        in_specs=[pl.BlockSpec((None, tr, d), lambda bi, i: (bi, i, 0)),
                  pl.BlockSpec((1, d), lambda bi, i: (0, 0)),
                  pl.BlockSpec((None, 1, d), lambda bi, i: (bi, 0, 0)),
                  pl.BlockSpec((None, 1, d), lambda bi, i: (bi, 0, 0))],
        out_specs=pl.BlockSpec((None, tr, d), lambda bi, i: (bi, i, 0)),
        out_shape=jax.ShapeDtypeStruct((b, s, d), BF16),
        compiler_params=_cparams(("parallel", "parallel")),
        name="norm_mod",
    )(x, g, scale, shift)


def _swap_quarters(v, even_quarter):
    quarter = QK_ROPE // 4
    return jnp.where(even_quarter, pltpu.roll(v, LANE - quarter, axis=1), pltpu.roll(v, quarter, axis=1))


def _qprep_body(z_ref, an_ref, w_ref, gn_ref, gr_ref, gs_ref, cos_ref, sin_ref, q_ref, *, n_heads, scale):
    x = z_ref[...]
    r = lax.rsqrt(jnp.mean(x * x, axis=-1, keepdims=True) + EPS)
    xn = (x * r * an_ref[...]).astype(BF16)
    acc = _dot(xn, w_ref[...])
    cos = cos_ref[...] * gr_ref[...]
    sin = sin_ref[...] * gs_ref[...]
    gn = gn_ref[...]
    hw = n_heads * LANE
    for h in range(n_heads):
        nope = acc[:, h * LANE:(h + 1) * LANE]
        rp = acc[:, hw + h * LANE: hw + (h + 1) * LANE]
        rs = acc[:, 2 * hw + h * LANE: 2 * hw + (h + 1) * LANE]
        ss = jnp.sum(nope * nope, axis=-1, keepdims=True) + jnp.sum(rp * rp, axis=-1, keepdims=True)
        inv = lax.rsqrt(ss * (1.0 / QK_DIM) + EPS) * scale
        q_ref[h, :, 0:LANE] = (nope * inv * gn).astype(q_ref.dtype)
        q_ref[h, :, LANE:HEAD_PAD] = ((rp * cos + rs * sin) * inv).astype(q_ref.dtype)


def _qprep(zq, a_norm, w_p, gn, gr, gs, cos, sin, n_heads):
    b, s, ql = zq.shape
    tr = _tile(s, 256)
    nw = w_p.shape[1]
    return pl.pallas_call(
        functools.partial(_qprep_body, n_heads=n_heads, scale=QK_DIM ** -0.5),
        grid=(b, s // tr),
        in_specs=[pl.BlockSpec((None, tr, ql), lambda bi, i: (bi, i, 0)),
                  pl.BlockSpec((1, ql), lambda bi, i: (0, 0)),
                  pl.BlockSpec((ql, nw), lambda bi, i: (0, 0)),
                  pl.BlockSpec((1, LANE), lambda bi, i: (0, 0)),
                  pl.BlockSpec((1, LANE), lambda bi, i: (0, 0)),
                  pl.BlockSpec((1, LANE), lambda bi, i: (0, 0)),
                  pl.BlockSpec((tr, LANE), lambda bi, i: (i, 0)),
                  pl.BlockSpec((tr, LANE), lambda bi, i: (i, 0))],
        out_specs=pl.BlockSpec((None, n_heads, tr, HEAD_PAD), lambda bi, i: (bi, 0, i, 0)),
        out_shape=jax.ShapeDtypeStruct((b, n_heads, s, HEAD_PAD), BF16),
        compiler_params=_cparams(("parallel", "parallel")),
        name="q_prep",
    )(zq, a_norm, w_p, gn, gr, gs, cos, sin)


def _kvprep_body(z_ref, an_ref, w_ref, gn_ref, gr_ref, gs_ref, cos_ref, sin_ref, k_ref, v_ref, *,
                 n_heads, kv_lora):
    x = z_ref[:, 0:kv_lora]
    tr = x.shape[0]
    lane = lax.broadcasted_iota(jnp.int32, (tr, LANE), 1)
    kr = jnp.where(lane < QK_ROPE, z_ref[:, kv_lora:kv_lora + LANE], 0.0)
    krs = _swap_quarters(kr, (lane // (QK_ROPE // 4)) % 2 == 0)
    r = lax.rsqrt(jnp.mean(x * x, axis=-1, keepdims=True) + EPS)
    xn = (x * r * an_ref[...]).astype(BF16)
    acc = _dot(xn, w_ref[...])
    ss_r = jnp.sum(kr * kr, axis=-1, keepdims=True)
    rope = kr * (cos_ref[...] * gr_ref[...]) + krs * (sin_ref[...] * gs_ref[...])
    gn = gn_ref[...]
    hw = n_heads * LANE
    ones_col = (lane == 0).astype(v_ref.dtype)
    for h in range(n_heads):
        kn = acc[:, h * LANE:(h + 1) * LANE]
        ss = jnp.sum(kn * kn, axis=-1, keepdims=True) + ss_r
        inv = lax.rsqrt(ss * (1.0 / QK_DIM) + EPS)
        k_ref[h, :, 0:LANE] = (kn * inv * gn).astype(k_ref.dtype)
        k_ref[h, :, LANE:HEAD_PAD] = (rope * inv).astype(k_ref.dtype)
        v_ref[h, :, 0:LANE] = acc[:, hw + h * LANE: hw + (h + 1) * LANE].astype(v_ref.dtype)
        v_ref[h, :, LANE:HEAD_PAD] = ones_col


def _kvprep(zkv, a_norm, w_p, gn, gr, gs, cos, sin, n_heads, kv_lora):
    b, s, zw = zkv.shape
    tr = _tile(s, 256)
    nw = w_p.shape[1]
    out = jax.ShapeDtypeStruct((b, n_heads, s, HEAD_PAD), BF16)
    ospec = pl.BlockSpec((None, n_heads, tr, HEAD_PAD), lambda bi, i: (bi, 0, i, 0))
    return pl.pallas_call(
        functools.partial(_kvprep_body, n_heads=n_heads, kv_lora=kv_lora),
        grid=(b, s // tr),
        in_specs=[pl.BlockSpec((None, tr, zw), lambda bi, i: (bi, i, 0)),
                  pl.BlockSpec((1, kv_lora), lambda bi, i: (0, 0)),
                  pl.BlockSpec((kv_lora, nw), lambda bi, i: (0, 0)),
                  pl.BlockSpec((1, LANE), lambda bi, i: (0, 0)),
                  pl.BlockSpec((1, LANE), lambda bi, i: (0, 0)),
                  pl.BlockSpec((1, LANE), lambda bi, i: (0, 0)),
                  pl.BlockSpec((tr, LANE), lambda bi, i: (i, 0)),
                  pl.BlockSpec((tr, LANE), lambda bi, i: (i, 0))],
        out_specs=[ospec, ospec],
        out_shape=[out, out],
        compiler_params=_cparams(("parallel", "parallel")),
        name="kv_prep",
    )(zkv, a_norm, w_p, gn, gr, gs, cos, sin)


def _attn_body(*refs, n_src, chunks):
    q_ref = refs[0]
    kv = refs[1:1 + 2 * n_src]
    gate_ref = refs[1 + 2 * n_src]
    o_ref = refs[2 + 2 * n_src]
    q = q_ref[...]
    tq = q.shape[0]
    plan = []
    for si in range(n_src):
        tk, n_chunks = chunks[si]
        plan += [(kv[2 * si], kv[2 * si + 1], ci * tk, tk) for ci in range(n_chunks)]

    def scores(i):
        k_ref, _, start, tk = plan[i]
        return lax.dot_general(q, k_ref[start:start + tk, :], (((1,), (1,)), ((), ())),
                               preferred_element_type=F32)

    m = jnp.full((tq, LANE), -jnp.inf, F32)
    acc = jnp.zeros((tq, HEAD_PAD), F32)
    s_next = scores(0)
    for i in range(len(plan)):
        s = s_next
        if i + 1 < len(plan):
            s_next = scores(i + 1)
        _, v_ref, start, tk = plan[i]
        m_new = jnp.maximum(m, jnp.max(s, axis=-1, keepdims=True))
        p = jnp.exp((s - jnp.tile(m_new, (1, tk // LANE))).astype(BF16))
        alpha = jnp.exp(m - m_new)
        acc = acc * jnp.tile(alpha, (1, HEAD_PAD // LANE)) + _dot(p, v_ref[start:start + tk, :])
        m = m_new
    out = acc[:, 0:V_DIM] / acc[:, V_DIM:V_DIM + 1]
    o_ref[...] = (out * gate_ref[...].astype(F32)).astype(o_ref.dtype)


def _attention(q, sources, gates, gate_col, n_heads):
    b, _, s, _ = q.shape
    tq = _tile(s, ATTN_TQ)
    in_specs = [pl.BlockSpec((None, None, tq, HEAD_PAD), lambda bi, h, i: (bi, h, i, 0))]
    args = [q]
    chunks = []
    for k, v in sources:
        sk = k.shape[2]
        tk = _tile(sk, ATTN_TK)
        chunks.append((tk, sk // tk))
        spec = pl.BlockSpec((None, None, sk, HEAD_PAD), lambda bi, h, i: (bi, h, 0, 0))
        in_specs += [spec, spec]
        args += [k, v]
    gb = gate_col // V_DIM
    in_specs.append(pl.BlockSpec((None, tq, V_DIM), lambda bi, h, i: (bi, i, gb + h)))
    args.append(gates)
    return pl.pallas_call(
        functools.partial(_attn_body, n_src=len(sources), chunks=tuple(chunks)),
        grid=(b, n_heads, s // tq),
        in_specs=in_specs,
        out_specs=pl.BlockSpec((None, tq, V_DIM), lambda bi, h, i: (bi, i, h)),
        out_shape=jax.ShapeDtypeStruct((b, s, n_heads * V_DIM), BF16),
        compiler_params=_cparams(("parallel", "parallel", "arbitrary")),
        name="attention",
    )(*args)


def _fft_factors(s):
    n1 = 128 if s % 128 == 0 and s // 128 >= 16 else 16
    return n1, s // n1


@functools.lru_cache(maxsize=None)
def _fft_tables(s, c):
    n1, n2 = _fft_factors(s)
    k1 = np.arange(n1, dtype=np.float64)
    a1 = 2.0 * np.pi * np.outer(k1, k1) / n1
    t1 = np.concatenate([np.cos(a1), -np.sin(a1)], axis=0)
    at = 2.0 * np.pi * np.outer(np.arange(n2, dtype=np.float64), k1) / s
    twr = np.cos(at)[:, :, None]
    twi = (-np.sin(at))[:, :, None]
    k2 = np.arange(n2, dtype=np.float64)
    a3 = 2.0 * np.pi * np.outer(k2, k2) / n2
    c3, s3 = np.cos(a3), np.sin(a3)
    t3 = np.block([[c3, s3], [-s3, c3]])
    kc = np.arange(c, dtype=np.float64)
    ac = 2.0 * np.pi * np.outer(kc, kc) / c
    tc = np.stack([np.cos(ac), np.sin(ac)], axis=0) / math.sqrt(float(s) * float(c))

    bf = lambda t: t.astype(np.float32).astype(BF16)
    return dict(t1=bf(t1), twr=twr.astype(np.float32), twi=twi.astype(np.float32), t3=bf(t3), tc=bf(tc))


def _fft1_body(u_ref, t_ref, twr_ref, twi_ref, y_ref, *, n1):
    y = _dot(t_ref[...], u_ref[...])
    yr, yi = y[0:n1], y[n1:2 * n1]
    tr, ti = twr_ref[...], twi_ref[...]
    y_ref[0] = (yr * tr - yi * ti).astype(y_ref.dtype)
    y_ref[1] = (yr * ti + yi * tr).astype(y_ref.dtype)


def _fft3_body(y_ref, t_ref, x_ref):
    x_ref[...] = _dot(t_ref[...], y_ref[...]).astype(x_ref.dtype)


def _fftc_body(x_ref, t_ref, wf_ref, gate_ref, o_ref):
    z = _dot(x_ref[0], t_ref[0]) + _dot(x_ref[1], t_ref[1])
    y = _dot(z.astype(BF16), wf_ref[...])
    o_ref[...] = (y * gate_ref[...].astype(F32)).astype(o_ref.dtype)


def _fourier_branch(u, w_fnet, gates, gate_col):
    b, s, c = u.shape
    n1, n2 = _fft_factors(s)
    tabs = _fft_tables(s, c)
    t1, t3, tc = (jnp.asarray(tabs[k]) for k in ("t1", "t3", "tc"))
    twr, twi = jnp.asarray(tabs["twr"]), jnp.asarray(tabs["twi"])

    const2 = lambda bi, j: (0, 0)
    y = pl.pallas_call(
        functools.partial(_fft1_body, n1=n1),
        grid=(b, n2),
        in_specs=[pl.BlockSpec((None, n1, c), lambda bi, j: (bi, 0, j)),
                  pl.BlockSpec((2 * n1, n1), const2),
                  pl.BlockSpec((None, n1, 1), lambda bi, j: (j, 0, 0)),
                  pl.BlockSpec((None, n1, 1), lambda bi, j: (j, 0, 0))],
        out_specs=pl.BlockSpec((None, 2, None, n1, c), lambda bi, j: (bi, 0, j, 0, 0)),
        out_shape=jax.ShapeDtypeStruct((b, 2, n2, n1, c), BF16),
        compiler_params=_cparams(("parallel", "parallel")),
        name="fft_stage1",
    )(u.reshape(b, n1, n2 * c), t1, twr, twi)

    cols = n1 * c
    tn = _tile(cols, 8192, LANE)
    x = pl.pallas_call(
        _fft3_body,
        grid=(b, cols // tn),
        in_specs=[pl.BlockSpec((None, 2 * n2, tn), lambda bi, j: (bi, 0, j)),
                  pl.BlockSpec((2 * n2, 2 * n2), const2)],
        out_specs=pl.BlockSpec((None, 2 * n2, tn), lambda bi, j: (bi, 0, j)),
        out_shape=jax.ShapeDtypeStruct((b, 2 * n2, cols), BF16),
        compiler_params=_cparams(("parallel", "parallel")),
        name="fft_stage2",
    )(y.reshape(b, 2 * n2, cols), t3)

    tm = _tile(s, 1024)
    gb = gate_col // c
    const3 = lambda bi, i: (0, 0, 0)
    return pl.pallas_call(
        _fftc_body,
        grid=(b, s // tm),
        in_specs=[pl.BlockSpec((None, 2, tm, c), lambda bi, i: (bi, 0, i, 0)),
                  pl.BlockSpec((2, c, c), const3),
                  pl.BlockSpec((c, c), lambda bi, i: (0, 0)),
                  pl.BlockSpec((None, tm, c), lambda bi, i: (bi, i, gb))],
        out_specs=pl.BlockSpec((None, tm, c), lambda bi, i: (bi, i, 0)),
        out_shape=jax.ShapeDtypeStruct((b, s, c), BF16),
        compiler_params=_cparams(("parallel", "parallel")),
        name="fft_channel",
    )(x.reshape(b, 2, s, c), tc, w_fnet, gates)


CONV_GROUP = 256


def _conv_body(hl_ref, hc_ref, hr_ref, wa_ref, wg_ref, cw_ref, cb_ref, lg_ref, lb_ref, w_ref, gate_ref, o_ref,
               ext_ref, sh_ref, acc_ref, *, c, ts, n_tiles):
    i = pl.program_id(1)
    h_ext = jnp.concatenate([hl_ref[...], hc_ref[...], hr_ref[...]], axis=0)
    row = lax.broadcasted_iota(jnp.int32, (ts + 2 * CONV_HALO, 1), 0)
    inside = ((row >= CONV_HALO) | (i > 0)) & ((row < CONV_HALO + ts) | (i < n_tiles - 1))
    span = sh_ref.shape[1]
    off = CONV_HALO - CONV_W // 2
    nt = (((1,), (1,)), ((), ()))
    group = min(CONV_GROUP, c)
    for c0 in range(0, c, group):
        cs = slice(c0, c0 + group)
        za = lax.dot_general(h_ext, wa_ref[cs, :], nt, preferred_element_type=F32)
        zg = lax.dot_general(h_ext, wg_ref[cs, :], nt, preferred_element_type=F32)
        ext_ref[:, cs] = jnp.where(inside, za * _sigmoid(zg), 0.0)
        for r in range(1, SUBLANE):
            sh_ref[r - 1, :, cs] = ext_ref[r:r + span, cs]
        acc = jnp.zeros((ts, group), F32) + cb_ref[:, cs]
        for j in range(CONV_W):
            r, base = (off + j) % SUBLANE, (off + j) // SUBLANE * SUBLANE
            src = ext_ref[base:base + ts, cs] if r == 0 else sh_ref[r - 1, base:base + ts, cs]
            acc = acc + src * cw_ref[j:j + 1, cs]
        acc_ref[:, cs] = acc
    acc = acc_ref[...]
    mu = jnp.mean(acc, axis=-1, keepdims=True)
    d = acc - mu
    var = jnp.mean(d * d, axis=-1, keepdims=True)
    y = _silu(d * lax.rsqrt(var + EPS) * lg_ref[...] + lb_ref[...])
    out = _dot(y.astype(BF16), w_ref[...])
    o_ref[...] = (out * gate_ref[...].astype(F32)).astype(o_ref.dtype)


def _conv_branch(h, w_t, glu_rows, conv_w, conv_b, cln_g, cln_b, w_pw2, gates):
    b, s, d = h.shape
    start, c2 = glu_rows
    c = c2 // 2
    ts = _tile(s, 256, CONV_HALO)
    n_tiles = s // ts
    hb = ts // CONV_HALO
    n_hblk = s // CONV_HALO
    row = lambda bi, i: (0, 0)
    once = dict(pipeline_mode=pl.Buffered(1))
    ja = start // c
    return pl.pallas_call(
        functools.partial(_conv_body, c=c, ts=ts, n_tiles=n_tiles),
        grid=(b, n_tiles),
        in_specs=[pl.BlockSpec((None, CONV_HALO, d), lambda bi, i: (bi, jnp.maximum(i * hb - 1, 0), 0)),
                  pl.BlockSpec((None, ts, d), lambda bi, i: (bi, i, 0)),
                  pl.BlockSpec((None, CONV_HALO, d),
                               lambda bi, i: (bi, jnp.minimum((i + 1) * hb, n_hblk - 1), 0)),
                  pl.BlockSpec((c, d), lambda bi, i: (ja, 0), **once),
                  pl.BlockSpec((c, d), lambda bi, i: (ja + 1, 0), **once),
                  pl.BlockSpec((CONV_W, c), row),
                  pl.BlockSpec((1, c), row),
                  pl.BlockSpec((1, c), row),
                  pl.BlockSpec((1, c), row),
                  pl.BlockSpec((c, c), row, **once),
                  pl.BlockSpec((None, ts, c), lambda bi, i: (bi, i, 0))],
        out_specs=pl.BlockSpec((None, ts, c), lambda bi, i: (bi, i, 0)),
        out_shape=jax.ShapeDtypeStruct((b, s, c), BF16),
        scratch_shapes=[pltpu.VMEM((ts + 2 * CONV_HALO, c), F32),
                        pltpu.VMEM((SUBLANE - 1, ts + 2 * CONV_HALO - SUBLANE, c), F32),
                        pltpu.VMEM((ts, c), F32)],
        compiler_params=_cparams(("parallel", "arbitrary")),
        name="conv_branch",
    )(h, h, h, w_t, w_t, conv_w, conv_b, cln_g, cln_b, w_pw2, gates)


def _merge_body(yf_ref, ym_ref, yc_ref, wf_ref, wm_ref, wc_ref, gf_ref, gm_ref, gc_ref, o_ref):
    acc = gf_ref[...].astype(F32) * _dot(yf_ref[...], wf_ref[...])
    acc = acc + gm_ref[...].astype(F32) * _dot(ym_ref[...], wm_ref[...])
    acc = acc + gc_ref[...].astype(F32) * _dot(yc_ref[...], wc_ref[...])
    o_ref[...] = acc.astype(o_ref.dtype)


def _merge(yf, ym, yc, wf, wm, wc, g):
    m = yf.shape[0]
    d = wf.shape[1]
    tm = _tile(m, 512)
    tn = _tile(d, 1024, LANE)
    nj = d // tn
    yspec = lambda arr: pl.BlockSpec((tm, arr.shape[1]), lambda j, i: (i, 0))
    wspec = lambda arr: pl.BlockSpec((arr.shape[0], tn), lambda j, i: (0, j))
    gspec = lambda t: pl.BlockSpec((tm, tn), lambda j, i, t=t: (i, t * nj + j))
    return pl.pallas_call(
        _merge_body,
        grid=(nj, m // tm),
        in_specs=[yspec(yf), yspec(ym), yspec(yc), wspec(wf), wspec(wm), wspec(wc),
                  gspec(0), gspec(1), gspec(2)],
        out_specs=pl.BlockSpec((tm, tn), lambda j, i: (i, j)),
        out_shape=jax.ShapeDtypeStruct((m, d), BF16),
        compiler_params=_cparams(("parallel", "parallel")),
        name="merge",
    )(yf, ym, yc, wf, wm, wc, g, g, g)


def _rope_tables(n_tok):
    half = QK_ROPE // 4
    inv = ROPE_BASE ** (-jnp.arange(half, dtype=F32) / half)
    t = jnp.arange(n_tok, dtype=jnp.int32)
    rows = (t // GRID_W).astype(F32)[:, None] * inv[None, :]
    cols = (t % GRID_W).astype(F32)[:, None] * inv[None, :]
    cr, sr, cc, sc = jnp.cos(rows), jnp.sin(rows), jnp.cos(cols), jnp.sin(cols)
    pad1 = jnp.ones((n_tok, LANE - QK_ROPE), F32)
    pad0 = jnp.zeros((n_tok, LANE - QK_ROPE), F32)
    cos = jnp.concatenate([cr, cr, cc, cc, pad1], axis=1)
    sin = jnp.concatenate([-sr, sr, -sc, sc, pad0], axis=1)
    return cos, sin


def _swap_perm():
    q = QK_ROPE // 4
    return np.concatenate([np.arange(q, 2 * q), np.arange(0, q), np.arange(3 * q, 4 * q), np.arange(2 * q, 3 * q)])


def _pad_lanes(w, width=LANE):
    return jnp.pad(w, [(0, 0)] * (w.ndim - 1) + [(0, width - w.shape[-1])])


def _prep_layer(l, dims, w_in, w_uq, w_ukv, q_norm, k_norm, w_fnet, w_pw2, w_br_f, w_br_m, w_br_c, w_out):
    d, f, ql, kvl, h, c = dims
    mla = h * V_DIM
    off_fg = f
    off_q = 2 * f
    off_kv = off_q + ql
    off_kr = off_kv + kvl
    off_mg = off_kr + QK_ROPE
    off_glu = off_mg + mla
    off_cg = off_glu + 2 * c
    off_merge = off_cg + c
    perm = _swap_perm()
    w_u = _cast_rows(w_in, l, 0, f)
    w_fg = _cast_rows(w_in, l, off_fg, f)
    w_q = _cast_rows(w_in, l, off_q, ql)
    w_kv = _cast_rows(w_in, l, off_kv, kvl + LANE)
    n_in = w_in.shape[1]
    w_tail = _cast_rows(w_in, l, off_mg, n_in - off_mg)
    tail = dict(mg=(0, mla), glu=(off_glu - off_mg, 2 * c), cg=(off_cg - off_mg, c),
                merge=(off_merge - off_mg, n_in - off_merge))

    wq3 = w_uq[l].reshape(ql, h, QK_DIM)
    q_nope = wq3[:, :, :QK_NOPE].reshape(ql, h * LANE)
    q_rope = wq3[:, :, QK_NOPE:]
    w_uq_p = jnp.concatenate([q_nope, _pad_lanes(q_rope).reshape(ql, h * LANE),
                              _pad_lanes(q_rope[:, :, perm]).reshape(ql, h * LANE)], axis=1).astype(BF16)
    wkv3 = w_ukv[l].reshape(kvl, h, QK_NOPE + V_DIM)
    w_ukv_p = jnp.concatenate([wkv3[:, :, :QK_NOPE].reshape(kvl, h * LANE),
                               wkv3[:, :, QK_NOPE:].reshape(kvl, h * LANE)], axis=1).astype(BF16)

    def gains(g):
        return (g[None, :QK_NOPE], _pad_lanes(g[None, QK_NOPE:]), _pad_lanes(g[None, QK_NOPE:][:, perm]))

    return dict(w_u=w_u, w_fg=w_fg, w_q=w_q, w_kv=w_kv, w_tail=w_tail, tail=tail,
                w_uq=w_uq_p, w_ukv=w_ukv_p, qg=gains(q_norm[l]), kg=gains(k_norm[l]),
                w_fnet=w_fnet[l].astype(BF16), w_pw2=w_pw2[l].astype(BF16),
                w_br_f=w_br_f[l].astype(BF16), w_br_m=w_br_m[l].astype(BF16), w_br_c=w_br_c[l].astype(BF16),
                w_out=w_out[l].astype(BF16))


def _kv_stream(h2, bsz, s, w, kv_a_norm, cos, sin, n_heads, kv_lora):
    zkv = _mm(h2, w["w_kv"], b_t=True, out_dtype=F32, tn=kv_lora + LANE, name="in_proj_kv")
    gn, gr, gs = w["kg"]
    return _kvprep(zkv.reshape(bsz, s, -1), kv_a_norm, w["w_ukv"], gn, gr, gs, cos, sin, n_heads, kv_lora)


def _full_stream(x, h, kv_own, kv_ctx, gate, w, p, cos, sin, dims):
    d, f, ql, kvl, n_heads, c = dims
    bsz, s, _ = x.shape
    m = bsz * s
    h2 = h.reshape(m, d)
    wt, tail = w["w_tail"], w["tail"]
    nt = dict(b_t=True)
    zu = _mm(h2, w["w_u"], tn=1024, out_dtype=BF16, name="in_proj_u", **nt)
    gate_f = _mm(h2, w["w_fg"], tn=1024, out_dtype=BF16, epi=_silu, name="in_proj_gate_f", **nt)
    zq = _mm(h2, w["w_q"], tn=ql, out_dtype=F32, name="in_proj_q", **nt)
    gate_m = _mm(h2, wt, b_cols=tail["mg"], tn=1024, out_dtype=BF16, epi=_silu, name="in_proj_gate_m", **nt)
    gate_c = _mm(h2, wt, b_cols=tail["cg"], tn=1024, out_dtype=BF16, epi=_silu, name="in_proj_gate_c", **nt)
    gmerge = _mm(h2, wt, b_cols=tail["merge"], tn=1024, out_dtype=BF16, epi=_sigmoid, name="in_proj_merge",
                 **nt)

    gn, gr, gs = w["qg"]
    q = _qprep(zq.reshape(bsz, s, ql), p["q_a_norm"], w["w_uq"], gn, gr, gs, cos, sin, n_heads)
    sources = ([kv_ctx] if kv_ctx is not None else []) + [kv_own]
    y_m = _attention(q, sources, gate_m.reshape(bsz, s, -1), 0, n_heads)
    y_f = _fourier_branch(zu.reshape(bsz, s, f), w["w_fnet"], gate_f.reshape(bsz, s, f), 0)
    y_c = _conv_branch(h, wt, tail["glu"], p["conv_w"], p["conv_b"], p["cln_g"], p["cln_b"],
                       w["w_pw2"], gate_c.reshape(bsz, s, c))
    merged = _merge(y_f.reshape(m, f), y_m.reshape(m, -1), y_c.reshape(m, c),
                    w["w_br_f"], w["w_br_m"], w["w_br_c"], gmerge)
    out = _mm(merged, w["w_out"], out_dtype=F32, rows_per_batch=s,
              epi=lambda acc, xv, gv: xv + gv * acc,
              extras=(("mn", x.reshape(m, d)), ("bn", gate)), name="out_proj")
    return out.reshape(bsz, s, d)


def kernel(x, c, ctx, c_ctx, norm_g, w_ada, b_ada, w_in, q_a_norm, w_uq, kv_a_norm, w_ukv, q_norm, k_norm,
           w_fnet, conv_w, conv_b, cln_g, cln_b, w_pw2, w_br_f, w_br_m, w_br_c, w_out):
    bsz, n_tok, d = x.shape
    n_ctx = ctx.shape[1]
    depth = w_in.shape[0]
    f = w_fnet.shape[1]
    ql = q_a_norm.shape[1]
    kvl = kv_a_norm.shape[1]
    n_heads = w_uq.shape[2] // QK_DIM
    cdim = conv_b.shape[1]
    dims = (d, f, ql, kvl, n_heads, cdim)

    cos_l, sin_l = _rope_tables(n_tok)
    cos_c = jnp.ones((n_ctx, LANE), F32)
    sin_c = jnp.zeros((n_ctx, LANE), F32)

    n_rows = -(-(bsz + 1) // 8) * 8
    cond = jnp.concatenate([c, c_ctx[None, :], jnp.zeros((n_rows - bsz - 1, d), F32)], axis=0)

    w_in_t = jnp.swapaxes(w_in, 1, 2)

    xl, xc = x, ctx
    for l in range(depth):
        last = l == depth - 1
        w = _prep_layer(l, dims, w_in_t, w_uq, w_ukv, q_norm, k_norm, w_fnet, w_pw2, w_br_f, w_br_m, w_br_c, w_out)
        p = dict(q_a_norm=q_a_norm[l][None], conv_w=conv_w[l], conv_b=conv_b[l][None],
                 cln_g=cln_g[l][None], cln_b=cln_b[l][None])
        mod = _mm(cond, w_ada, b_layer=l, out_dtype=F32, tm=n_rows, tn=1024, tk=2048, a_act=_silu,
                  epi=lambda acc, bias: acc + bias, extras=(("n", b_ada[l][None]),), name="adaln")
        shift_l, scale_l, gate_l = (mod[:bsz, i * d:(i + 1) * d][:, None, :] for i in range(3))
        shift_c, scale_c, gate_c = (jnp.broadcast_to(mod[bsz, i * d:(i + 1) * d][None, None, :], (bsz, 1, d))
                                    for i in range(3))
        g = norm_g[l][None]
        hl = _norm_mod(xl, g, scale_l, shift_l)
        hc = _norm_mod(xc, g, scale_c, shift_c)
        kv_c = _kv_stream(hc.reshape(bsz * n_ctx, d), bsz, n_ctx, w, kv_a_norm[l][None], cos_c, sin_c, n_heads, kvl)
        kv_l = _kv_stream(hl.reshape(bsz * n_tok, d), bsz, n_tok, w, kv_a_norm[l][None], cos_l, sin_l, n_heads, kvl)
        new_xl = _full_stream(xl, hl, kv_l, kv_c, gate_l, w, p, cos_l, sin_l, dims)
        if not last:
            xc = _full_stream(xc, hc, kv_c, None, gate_c, w, p, cos_c, sin_c, dims)
        xl = new_xl
    return xl
```

```python
import functools
import math

import numpy as np
import jax
import jax.numpy as jnp
from jax import lax
from jax.experimental import pallas as pl
from jax.experimental.pallas import tpu as pltpu

F32 = jnp.float32
BF16 = jnp.bfloat16

EPS = 1e-6
QK_NOPE = 128
QK_ROPE = 64
V_DIM = 128
QK_DIM = QK_NOPE + QK_ROPE
HEAD_PAD = 256
LANE = 128
SUBLANE = 8
GRID_W = 64
CONV_W = 31
CONV_HALO = 16
ROPE_BASE = 10000.0
N_BRANCH = 3
ATTN_TQ = 1024
ATTN_TK = 2048
VMEM_LIMIT = 52 * 1024 * 1024


def _cparams(sem):
    return pltpu.CompilerParams(dimension_semantics=sem, vmem_limit_bytes=VMEM_LIMIT)


def _tile(n, pref, align=8):
    if n <= pref:
        return n
    t = (pref // align) * align
    while t >= align:
        if n % t == 0:
            return t
        t -= align
    return n


def _sigmoid(v):
    return 0.5 * jnp.tanh(0.5 * v) + 0.5


def _silu(v):
    return v * _sigmoid(v)


def _dot(a, b):
    return jnp.dot(a, b, preferred_element_type=F32)


MM_SUB = 256


def _mm_body(*refs, nk, n_extra, epi, a_act, b_t):
    a_ref, b_ref = refs[0], refs[1]
    extra = refs[2:2 + n_extra]
    o_ref = refs[2 + n_extra]
    a = a_ref[...]
    if a_act is not None:
        a = a_act(a.astype(F32))
    a = a.astype(BF16)
    tn = o_ref.shape[-1]

    def prod(c0, width):
        if b_t:
            return lax.dot_general(a, b_ref[c0:c0 + width, :].astype(BF16), (((1,), (1,)), ((), ())),
                                   preferred_element_type=F32)
        return _dot(a, b_ref[:, c0:c0 + width].astype(BF16))

    if nk == 1:
        sub = MM_SUB if tn % MM_SUB == 0 else tn
        for c0 in range(0, tn, sub):
            p = prod(c0, sub)
            o_ref[:, c0:c0 + sub] = epi(p, *[e[:, c0:c0 + sub] for e in extra]).astype(o_ref.dtype)
    else:
        p = prod(0, tn)
        acc_ref = refs[3 + n_extra]
        k = pl.program_id(2)

        @pl.when(k == 0)
        def _():
            acc_ref[...] = p

        @pl.when(k > 0)
        def _():
            acc_ref[...] += p

        @pl.when(k == nk - 1)
        def _():
            o_ref[...] = epi(acc_ref[...], *[e[...] for e in extra]).astype(o_ref.dtype)


def _mm(a, b, *, out_dtype, tm=1024, tn=512, tk=4096, epi=None, extras=(), a_act=None,
        rows_per_batch=None, b_cols=None, b_layer=None, b_t=False, name="mm"):
    m, kdim = a.shape
    col0, n = (0, b.shape[-2 if b_t else -1]) if b_cols is None else b_cols
    tm = _tile(m, tm)
    if rows_per_batch is not None:
        tm = _tile(rows_per_batch, tm)
    tn = _tile(n, tn, LANE)
    while col0 % tn:
        tn = _tile(n, tn - LANE, LANE)
    tk = _tile(kdim, tk, LANE)
    nk = kdim // tk
    jb = col0 // tn
    if epi is None:
        epi = lambda p: p
    lead = (None,) if b.ndim == 3 else ()
    lidx = (b_layer,) if b.ndim == 3 else ()
    if b_t:
        b_spec = pl.BlockSpec(lead + (tn, tk), lambda i, j, k: lidx + (j + jb, k))
    else:
        b_spec = pl.BlockSpec(lead + (tk, tn), lambda i, j, k: lidx + (k, j + jb))
    in_specs = [pl.BlockSpec((tm, tk), lambda i, j, k: (i, k)), b_spec]
    args = [a, b]
    for kind, arr in extras:
        if kind == "mn":
            in_specs.append(pl.BlockSpec((tm, tn), lambda i, j, k: (i, j)))
        elif kind == "n":
            in_specs.append(pl.BlockSpec((1, tn), lambda i, j, k: (0, j)))
        else:
            bpt = rows_per_batch // tm
            in_specs.append(pl.BlockSpec((None, 1, tn), lambda i, j, k, bpt=bpt: (i // bpt, 0, j)))
        args.append(arr)
    scratch = [pltpu.VMEM((tm, tn), F32)] if nk > 1 else []
    return pl.pallas_call(
        functools.partial(_mm_body, nk=nk, n_extra=len(extras), epi=epi, a_act=a_act, b_t=b_t),
        grid=(m // tm, n // tn, nk),
        in_specs=in_specs,
        out_specs=pl.BlockSpec((tm, tn), lambda i, j, k: (i, j)),
        out_shape=jax.ShapeDtypeStruct((m, n), out_dtype),
        scratch_shapes=scratch,
        compiler_params=_cparams(("parallel", "parallel", "arbitrary")),
        name=name,
    )(*args)


def _cast_rows_body(w_ref, o_ref):
    o_ref[...] = w_ref[...].astype(o_ref.dtype)


def _cast_rows_offset_body(main_ref, next_ref, o_ref):
    half = next_ref.shape[0]
    rb = main_ref.shape[0]
    o_ref[0:rb - half, :] = main_ref[half:rb, :].astype(o_ref.dtype)
    o_ref[rb - half:rb, :] = next_ref[...].astype(o_ref.dtype)


def _cast_rows(w, l, start, n_rows):
    kdim = w.shape[2]
    half = LANE // 2
    if start % LANE == half and n_rows % LANE == 0:
        base = start - half
        rb = 2 * LANE if base % (2 * LANE) == 0 and n_rows % (2 * LANE) == 0 else LANE
        jb, r = base // rb, rb // half
        return pl.pallas_call(
            _cast_rows_offset_body,
            grid=(n_rows // rb,),
            in_specs=[pl.BlockSpec((None, rb, kdim), lambda j: (l, jb + j, 0)),
                      pl.BlockSpec((None, half, kdim), lambda j: (l, (jb + j + 1) * r, 0))],
            out_specs=pl.BlockSpec((rb, kdim), lambda j: (j, 0)),
            out_shape=jax.ShapeDtypeStruct((n_rows, kdim), BF16),
            compiler_params=_cparams(("parallel",)),
            name="cast_rows_offset",
        )(w, w)
    rb = 512
    while start % rb or n_rows % rb:
        rb //= 2
    jb = start // rb
    return pl.pallas_call(
        _cast_rows_body,
        grid=(n_rows // rb,),
        in_specs=[pl.BlockSpec((None, rb, kdim), lambda j: (l, jb + j, 0))],
        out_specs=pl.BlockSpec((rb, kdim), lambda j: (j, 0)),
        out_shape=jax.ShapeDtypeStruct((n_rows, kdim), BF16),
        compiler_params=_cparams(("parallel",)),
        name="cast_rows",
    )(w)


def _norm_body(x_ref, g_ref, scale_ref, shift_ref, o_ref):
    x = x_ref[...]
    r = lax.rsqrt(jnp.mean(x * x, axis=-1, keepdims=True) + EPS)
    y = x * r * g_ref[...]
    o_ref[...] = (y * (1.0 + scale_ref[...]) + shift_ref[...]).astype(o_ref.dtype)


def _norm_mod(x, g, scale, shift):
    b, s, d = x.shape
    tr = _tile(s, 256)
    return pl.pallas_call(
        _norm_body,
        grid=(b, s // tr),
        in_specs=[pl.BlockSpec((None, tr, d), lambda bi, i: (bi, i, 0)),
                  pl.BlockSpec((1, d), lambda bi, i: (0, 0)),
                  pl.BlockSpec((None, 1, d), lambda bi, i: (bi, 0, 0)),
                  pl.BlockSpec((None, 1, d), lambda bi, i: (bi, 0, 0))],
        out_specs=pl.BlockSpec((None, tr, d), lambda bi, i: (bi, i, 0)),
        out_shape=jax.ShapeDtypeStruct((b, s, d), BF16),
        compiler_params=_cparams(("parallel", "parallel")),
        name="norm_mod",
    )(x, g, scale, shift)


def _swap_quarters(v, even_quarter):
    quarter = QK_ROPE // 4
    return jnp.where(even_quarter, pltpu.roll(v, LANE - quarter, axis=1), pltpu.roll(v, quarter, axis=1))


_NT = (((1,), (1,)), ((), ()))


def _qprep_body(z_ref, an_ref, w_ref, gn_ref, gr_ref, gs_ref, cos_ref, sin_ref, q_ref, *, n_heads, scale):
    x = z_ref[...]
    r = lax.rsqrt(jnp.mean(x * x, axis=-1, keepdims=True) + EPS)
    xn = (x * r * an_ref[...]).astype(BF16)
    acc = _dot(xn, w_ref[...])
    cos = cos_ref[...] * gr_ref[...]
    sin = sin_ref[...] * gs_ref[...]
    gn = gn_ref[...]
    hw = n_heads * LANE
    for h in range(n_heads):
        nope = acc[:, h * LANE:(h + 1) * LANE]
        rp = acc[:, hw + h * LANE: hw + (h + 1) * LANE]
        rs = acc[:, 2 * hw + h * LANE: 2 * hw + (h + 1) * LANE]
        ss = jnp.sum(nope * nope + rp * rp, axis=-1, keepdims=True)
        inv = lax.rsqrt(ss * (1.0 / QK_DIM) + EPS) * scale
        q_ref[h, :, 0:LANE] = (nope * inv * gn).astype(q_ref.dtype)
        q_ref[h, :, LANE:HEAD_PAD] = ((rp * cos + rs * sin) * inv).astype(q_ref.dtype)


def _qprep(zq, a_norm, w_p, gn, gr, gs, cos, sin, n_heads):
    b, s, ql = zq.shape
    tr = _tile(s, 512)
    nw = w_p.shape[1]
    return pl.pallas_call(
        functools.partial(_qprep_body, n_heads=n_heads, scale=QK_DIM ** -0.5),
        grid=(b, s // tr),
        in_specs=[pl.BlockSpec((None, tr, ql), lambda bi, i: (bi, i, 0)),
                  pl.BlockSpec((1, ql), lambda bi, i: (0, 0)),
                  pl.BlockSpec((ql, nw), lambda bi, i: (0, 0), pipeline_mode=pl.Buffered(1)),
                  pl.BlockSpec((1, LANE), lambda bi, i: (0, 0)),
                  pl.BlockSpec((1, LANE), lambda bi, i: (0, 0)),
                  pl.BlockSpec((1, LANE), lambda bi, i: (0, 0)),
                  pl.BlockSpec((tr, LANE), lambda bi, i: (i, 0)),
                  pl.BlockSpec((tr, LANE), lambda bi, i: (i, 0))],
        out_specs=pl.BlockSpec((None, n_heads, tr, HEAD_PAD), lambda bi, i: (bi, 0, i, 0)),
        out_shape=jax.ShapeDtypeStruct((b, n_heads, s, HEAD_PAD), BF16),
        compiler_params=_cparams(("parallel", "parallel")),
        name="q_prep",
    )(zq, a_norm, w_p, gn, gr, gs, cos, sin)


def _kvprep_body(h_ref, wkv_ref, an_ref, w_ref, gn_ref, gr_ref, gs_ref, cos_ref, sin_ref, k_ref, v_ref, *,
                 n_heads, kv_lora):
    z = lax.dot_general(h_ref[...], wkv_ref[...], _NT, preferred_element_type=F32)
    x = z[:, 0:kv_lora]
    tr = x.shape[0]
    lane = lax.broadcasted_iota(jnp.int32, (tr, LANE), 1)
    kr = jnp.where(lane < QK_ROPE, z[:, kv_lora:kv_lora + LANE], 0.0)
    krs = _swap_quarters(kr, (lane // (QK_ROPE // 4)) % 2 == 0)
    r = lax.rsqrt(jnp.mean(x * x, axis=-1, keepdims=True) + EPS)
    xn = (x * r * an_ref[...]).astype(BF16)
    acc = _dot(xn, w_ref[...])
    ss_r = jnp.sum(kr * kr, axis=-1, keepdims=True)
    rope = kr * (cos_ref[...] * gr_ref[...]) + krs * (sin_ref[...] * gs_ref[...])
    gn = gn_ref[...]
    hw = n_heads * LANE
    ones_col = (lane == 0).astype(v_ref.dtype)
    for h in range(n_heads):
        kn = acc[:, h * LANE:(h + 1) * LANE]
        ss = jnp.sum(kn * kn, axis=-1, keepdims=True) + ss_r
        inv = lax.rsqrt(ss * (1.0 / QK_DIM) + EPS)
        k_ref[h, :, 0:LANE] = (kn * inv * gn).astype(k_ref.dtype)
        k_ref[h, :, LANE:HEAD_PAD] = (rope * inv).astype(k_ref.dtype)
        v_ref[h, :, 0:LANE] = acc[:, hw + h * LANE: hw + (h + 1) * LANE].astype(v_ref.dtype)
        v_ref[h, :, LANE:HEAD_PAD] = ones_col


def _kvprep(h, w_kv, a_norm, w_p, gn, gr, gs, cos, sin, n_heads, kv_lora):
    b, s, d = h.shape
    zw = w_kv.shape[0]
    tr = _tile(s, 256)
    nw = w_p.shape[1]
    out = jax.ShapeDtypeStruct((b, n_heads, s, HEAD_PAD), BF16)
    ospec = pl.BlockSpec((None, n_heads, tr, HEAD_PAD), lambda bi, i: (bi, 0, i, 0))
    once = dict(pipeline_mode=pl.Buffered(1))
    return pl.pallas_call(
        functools.partial(_kvprep_body, n_heads=n_heads, kv_lora=kv_lora),
        grid=(b, s // tr),
        in_specs=[pl.BlockSpec((None, tr, d), lambda bi, i: (bi, i, 0)),
                  pl.BlockSpec((zw, d), lambda bi, i: (0, 0), **once),
                  pl.BlockSpec((1, kv_lora), lambda bi, i: (0, 0)),
                  pl.BlockSpec((kv_lora, nw), lambda bi, i: (0, 0), **once),
                  pl.BlockSpec((1, LANE), lambda bi, i: (0, 0)),
                  pl.BlockSpec((1, LANE), lambda bi, i: (0, 0)),
                  pl.BlockSpec((1, LANE), lambda bi, i: (0, 0)),
                  pl.BlockSpec((tr, LANE), lambda bi, i: (i, 0)),
                  pl.BlockSpec((tr, LANE), lambda bi, i: (i, 0))],
        out_specs=[ospec, ospec],
        out_shape=[out, out],
        compiler_params=_cparams(("parallel", "parallel")),
        name="kv_prep",
    )(h, w_kv, a_norm, w_p, gn, gr, gs, cos, sin)


def _attn_body(*refs, n_src, chunks):
    q_ref = refs[0]
    kv = refs[1:1 + 2 * n_src]
    gate_ref = refs[1 + 2 * n_src]
    o_ref = refs[2 + 2 * n_src]
    q = q_ref[...]
    tq = q.shape[0]
    plan = []
    for si in range(n_src):
        tk, n_chunks = chunks[si]
        plan += [(kv[2 * si], kv[2 * si + 1], ci * tk, tk) for ci in range(n_chunks)]

    def scores(i):
        k_ref, _, start, tk = plan[i]
        return lax.dot_general(q, k_ref[start:start + tk, :], (((1,), (1,)), ((), ())),
                               preferred_element_type=F32)

    m = jnp.full((tq, LANE), -jnp.inf, F32)
    acc = jnp.zeros((tq, HEAD_PAD), F32)
    s_next = scores(0)
    for i in range(len(plan)):
        s = s_next
        if i + 1 < len(plan):
            s_next = scores(i + 1)
        _, v_ref, start, tk = plan[i]
        m_new = jnp.maximum(m, jnp.max(s, axis=-1, keepdims=True))
        p = jnp.exp((s - jnp.tile(m_new, (1, tk // LANE))).astype(BF16))
        alpha = jnp.exp(m - m_new)
        acc = acc * jnp.tile(alpha, (1, HEAD_PAD // LANE)) + _dot(p, v_ref[start:start + tk, :])
        m = m_new
    out = acc[:, 0:V_DIM] / acc[:, V_DIM:V_DIM + 1]
    o_ref[...] = (out * gate_ref[...].astype(F32)).astype(o_ref.dtype)


def _attention(q, sources, gates, gate_col, n_heads):
    b, _, s, _ = q.shape
    tq = _tile(s, ATTN_TQ)
    in_specs = [pl.BlockSpec((None, None, tq, HEAD_PAD), lambda bi, h, i: (bi, h, i, 0))]
    args = [q]
    chunks = []
    for k, v in sources:
        sk = k.shape[2]
        tk = _tile(sk, ATTN_TK)
        chunks.append((tk, sk // tk))
        spec = pl.BlockSpec((None, None, sk, HEAD_PAD), lambda bi, h, i: (bi, h, 0, 0))
        in_specs += [spec, spec]
        args += [k, v]
    gb = gate_col // V_DIM
    in_specs.append(pl.BlockSpec((None, tq, V_DIM), lambda bi, h, i: (bi, i, gb + h)))
    args.append(gates)
    return pl.pallas_call(
        functools.partial(_attn_body, n_src=len(sources), chunks=tuple(chunks)),
        grid=(b, n_heads, s // tq),
        in_specs=in_specs,
        out_specs=pl.BlockSpec((None, tq, V_DIM), lambda bi, h, i: (bi, i, h)),
        out_shape=jax.ShapeDtypeStruct((b, s, n_heads * V_DIM), BF16),
        compiler_params=_cparams(("parallel", "parallel", "arbitrary")),
        name="attention",
    )(*args)


def _fft_factors(s):
    n1 = 128 if s % 128 == 0 and s // 128 >= 16 else 16
    return n1, s // n1


@functools.lru_cache(maxsize=None)
def _fft_tables(s, c):
    n1, n2 = _fft_factors(s)
    k1 = np.arange(n1, dtype=np.float64)
    a1 = 2.0 * np.pi * np.outer(k1, k1) / n1
    t1 = np.concatenate([np.cos(a1), -np.sin(a1)], axis=0)
    at = 2.0 * np.pi * np.outer(np.arange(n2, dtype=np.float64), k1) / s
    twr = np.cos(at)[:, :, None]
    twi = (-np.sin(at))[:, :, None]
    k2 = np.arange(n2, dtype=np.float64)
    a3 = 2.0 * np.pi * np.outer(k2, k2) / n2
    c3, s3 = np.cos(a3), np.sin(a3)
    t3 = np.block([[c3, s3], [-s3, c3]])
    kc = np.arange(c, dtype=np.float64)
    ac = 2.0 * np.pi * np.outer(kc, kc) / c
    tc = np.stack([np.cos(ac), np.sin(ac)], axis=0) / math.sqrt(float(s) * float(c))

    bf = lambda t: t.astype(np.float32).astype(BF16)
    return dict(t1=bf(t1), twr=twr.astype(np.float32), twi=twi.astype(np.float32), t3=bf(t3), tc=bf(tc))


def _fft1_body(u_ref, t_ref, twr_ref, twi_ref, y_ref, *, n1):
    y = _dot(t_ref[...], u_ref[...])
    yr, yi = y[0:n1], y[n1:2 * n1]
    tr, ti = twr_ref[...], twi_ref[...]
    y_ref[0] = (yr * tr - yi * ti).astype(y_ref.dtype)
    y_ref[1] = (yr * ti + yi * tr).astype(y_ref.dtype)


def _fft3_body(y_ref, t_ref, x_ref):
    x_ref[...] = _dot(t_ref[...], y_ref[...]).astype(x_ref.dtype)


def _fftc_body(x_ref, t_ref, wf_ref, gate_ref, o_ref):
    z = _dot(x_ref[0], t_ref[0]) + _dot(x_ref[1], t_ref[1])
    y = _dot(z.astype(BF16), wf_ref[...])
    o_ref[...] = (y * gate_ref[...].astype(F32)).astype(o_ref.dtype)


def _fourier_branch(u, w_fnet, gates, gate_col):
    b, s, c = u.shape
    n1, n2 = _fft_factors(s)
    tabs = _fft_tables(s, c)
    t1, t3, tc = (jnp.asarray(tabs[k]) for k in ("t1", "t3", "tc"))
    twr, twi = jnp.asarray(tabs["twr"]), jnp.asarray(tabs["twi"])

    const2 = lambda bi, j: (0, 0)
    y = pl.pallas_call(
        functools.partial(_fft1_body, n1=n1),
        grid=(b, n2),
        in_specs=[pl.BlockSpec((None, n1, c), lambda bi, j: (bi, 0, j)),
                  pl.BlockSpec((2 * n1, n1), const2),
                  pl.BlockSpec((None, n1, 1), lambda bi, j: (j, 0, 0)),
                  pl.BlockSpec((None, n1, 1), lambda bi, j: (j, 0, 0))],
        out_specs=pl.BlockSpec((None, 2, None, n1, c), lambda bi, j: (bi, 0, j, 0, 0)),
        out_shape=jax.ShapeDtypeStruct((b, 2, n2, n1, c), BF16),
        compiler_params=_cparams(("parallel", "parallel")),
        name="fft_stage1",
    )(u.reshape(b, n1, n2 * c), t1, twr, twi)

    cols = n1 * c
    tn = _tile(cols, 8192, LANE)
    x = pl.pallas_call(
        _fft3_body,
        grid=(b, cols // tn),
        in_specs=[pl.BlockSpec((None, 2 * n2, tn), lambda bi, j: (bi, 0, j)),
                  pl.BlockSpec((2 * n2, 2 * n2), const2)],
        out_specs=pl.BlockSpec((None, 2 * n2, tn), lambda bi, j: (bi, 0, j)),
        out_shape=jax.ShapeDtypeStruct((b, 2 * n2, cols), BF16),
        compiler_params=_cparams(("parallel", "parallel")),
        name="fft_stage2",
    )(y.reshape(b, 2 * n2, cols), t3)

    tm = _tile(s, 1024)
    gb = gate_col // c
    const3 = lambda bi, i: (0, 0, 0)
    return pl.pallas_call(
        _fftc_body,
        grid=(b, s // tm),
        in_specs=[pl.BlockSpec((None, 2, tm, c), lambda bi, i: (bi, 0, i, 0)),
                  pl.BlockSpec((2, c, c), const3),
                  pl.BlockSpec((c, c), lambda bi, i: (0, 0)),
                  pl.BlockSpec((None, tm, c), lambda bi, i: (bi, i, gb))],
        out_specs=pl.BlockSpec((None, tm, c), lambda bi, i: (bi, i, 0)),
        out_shape=jax.ShapeDtypeStruct((b, s, c), BF16),
        compiler_params=_cparams(("parallel", "parallel")),
        name="fft_channel",
    )(x.reshape(b, 2, s, c), tc, w_fnet, gates)


CONV_GROUP = 256


def _conv_body(hl_ref, hc_ref, hr_ref, wa_ref, wg_ref, cw_ref, cb_ref, lg_ref, lb_ref, w_ref, gate_ref, o_ref,
               ext_ref, sh_ref, acc_ref, *, c, ts, n_tiles):
    i = pl.program_id(1)
    h_ext = jnp.concatenate([hl_ref[...], hc_ref[...], hr_ref[...]], axis=0)
    row = lax.broadcasted_iota(jnp.int32, (ts + 2 * CONV_HALO, 1), 0)
    inside = ((row >= CONV_HALO) | (i > 0)) & ((row < CONV_HALO + ts) | (i < n_tiles - 1))
    span = sh_ref.shape[1]
    off = CONV_HALO - CONV_W // 2
    nt = (((1,), (1,)), ((), ()))
    group = min(CONV_GROUP, c)
    for c0 in range(0, c, group):
        cs = slice(c0, c0 + group)
        za = lax.dot_general(h_ext, wa_ref[cs, :], nt, preferred_element_type=F32)
        zg = lax.dot_general(h_ext, wg_ref[cs, :], nt, preferred_element_type=F32)
        ext_ref[:, cs] = jnp.where(inside, za * _sigmoid(zg), 0.0)
        for r in range(1, SUBLANE):
            sh_ref[r - 1, :, cs] = ext_ref[r:r + span, cs]
        acc = jnp.zeros((ts, group), F32) + cb_ref[:, cs]
        for j in range(CONV_W):
            r, base = (off + j) % SUBLANE, (off + j) // SUBLANE * SUBLANE
            src = ext_ref[base:base + ts, cs] if r == 0 else sh_ref[r - 1, base:base + ts, cs]
            acc = acc + src * cw_ref[j:j + 1, cs]
        acc_ref[:, cs] = acc
    acc = acc_ref[...]
    mu = jnp.mean(acc, axis=-1, keepdims=True)
    d = acc - mu
    var = jnp.mean(d * d, axis=-1, keepdims=True)
    y = _silu(d * lax.rsqrt(var + EPS) * lg_ref[...] + lb_ref[...])
    out = _dot(y.astype(BF16), w_ref[...])
    o_ref[...] = (out * gate_ref[...].astype(F32)).astype(o_ref.dtype)


def _conv_branch(h, w_t, glu_rows, conv_w, conv_b, cln_g, cln_b, w_pw2, gates):
    b, s, d = h.shape
    start, c2 = glu_rows
    c = c2 // 2
    ts = _tile(s, 256, CONV_HALO)
    n_tiles = s // ts
    hb = ts // CONV_HALO
    n_hblk = s // CONV_HALO
    row = lambda bi, i: (0, 0)
    once = dict(pipeline_mode=pl.Buffered(1))
    ja = start // c
    return pl.pallas_call(
        functools.partial(_conv_body, c=c, ts=ts, n_tiles=n_tiles),
        grid=(b, n_tiles),
        in_specs=[pl.BlockSpec((None, CONV_HALO, d), lambda bi, i: (bi, jnp.maximum(i * hb - 1, 0), 0)),
                  pl.BlockSpec((None, ts, d), lambda bi, i: (bi, i, 0)),
                  pl.BlockSpec((None, CONV_HALO, d),
                               lambda bi, i: (bi, jnp.minimum((i + 1) * hb, n_hblk - 1), 0)),
                  pl.BlockSpec((c, d), lambda bi, i: (ja, 0), **once),
                  pl.BlockSpec((c, d), lambda bi, i: (ja + 1, 0), **once),
                  pl.BlockSpec((CONV_W, c), row),
                  pl.BlockSpec((1, c), row),
                  pl.BlockSpec((1, c), row),
                  pl.BlockSpec((1, c), row),
                  pl.BlockSpec((c, c), row, **once),
                  pl.BlockSpec((None, ts, c), lambda bi, i: (bi, i, 0))],
        out_specs=pl.BlockSpec((None, ts, c), lambda bi, i: (bi, i, 0)),
        out_shape=jax.ShapeDtypeStruct((b, s, c), BF16),
        scratch_shapes=[pltpu.VMEM((ts + 2 * CONV_HALO, c), F32),
                        pltpu.VMEM((SUBLANE - 1, ts + 2 * CONV_HALO - SUBLANE, c), F32),
                        pltpu.VMEM((ts, c), F32)],
        compiler_params=_cparams(("parallel", "arbitrary")),
        name="conv_branch",
    )(h, h, h, w_t, w_t, conv_w, conv_b, cln_g, cln_b, w_pw2, gates)


def _merge_body(yf_ref, ym_ref, yc_ref, wf_ref, wm_ref, wc_ref, gf_ref, gm_ref, gc_ref, o_ref):
    yf, ym, yc = yf_ref[...], ym_ref[...], yc_ref[...]
    tn = o_ref.shape[-1]
    sub = MM_SUB if tn % MM_SUB == 0 else tn
    for c0 in range(0, tn, sub):
        cs = slice(c0, c0 + sub)
        acc = gf_ref[:, cs].astype(F32) * _dot(yf, wf_ref[:, cs])
        acc = acc + gm_ref[:, cs].astype(F32) * _dot(ym, wm_ref[:, cs])
        acc = acc + gc_ref[:, cs].astype(F32) * _dot(yc, wc_ref[:, cs])
        o_ref[:, cs] = acc.astype(o_ref.dtype)


def _merge(yf, ym, yc, wf, wm, wc, g):
    m = yf.shape[0]
    d = wf.shape[1]
    tm = _tile(m, 512)
    tn = _tile(d, 1024, LANE)
    nj = d // tn
    yspec = lambda arr: pl.BlockSpec((tm, arr.shape[1]), lambda j, i: (i, 0))
    wspec = lambda arr: pl.BlockSpec((arr.shape[0], tn), lambda j, i: (0, j))
    gspec = lambda t: pl.BlockSpec((tm, tn), lambda j, i, t=t: (i, t * nj + j))
    return pl.pallas_call(
        _merge_body,
        grid=(nj, m // tm),
        in_specs=[yspec(yf), yspec(ym), yspec(yc), wspec(wf), wspec(wm), wspec(wc),
                  gspec(0), gspec(1), gspec(2)],
        out_specs=pl.BlockSpec((tm, tn), lambda j, i: (i, j)),
        out_shape=jax.ShapeDtypeStruct((m, d), BF16),
        compiler_params=_cparams(("parallel", "parallel")),
        name="merge",
    )(yf, ym, yc, wf, wm, wc, g, g, g)


def _rope_tables(n_tok):
    half = QK_ROPE // 4
    inv = ROPE_BASE ** (-jnp.arange(half, dtype=F32) / half)
    t = jnp.arange(n_tok, dtype=jnp.int32)
    rows = (t // GRID_W).astype(F32)[:, None] * inv[None, :]
    cols = (t % GRID_W).astype(F32)[:, None] * inv[None, :]
    cr, sr, cc, sc = jnp.cos(rows), jnp.sin(rows), jnp.cos(cols), jnp.sin(cols)
    pad1 = jnp.ones((n_tok, LANE - QK_ROPE), F32)
    pad0 = jnp.zeros((n_tok, LANE - QK_ROPE), F32)
    cos = jnp.concatenate([cr, cr, cc, cc, pad1], axis=1)
    sin = jnp.concatenate([-sr, sr, -sc, sc, pad0], axis=1)
    return cos, sin


def _swap_perm():
    q = QK_ROPE // 4
    return np.concatenate([np.arange(q, 2 * q), np.arange(0, q), np.arange(3 * q, 4 * q), np.arange(2 * q, 3 * q)])


def _pad_lanes(w, width=LANE):
    return jnp.pad(w, [(0, 0)] * (w.ndim - 1) + [(0, width - w.shape[-1])])


def _prep_layer(l, dims, w_in, w_uq, w_ukv, q_norm, k_norm, w_fnet, w_pw2, w_br_f, w_br_m, w_br_c, w_out):
    d, f, ql, kvl, h, c = dims
    mla = h * V_DIM
    off_fg = f
    off_q = 2 * f
    off_kv = off_q + ql
    off_kr = off_kv + kvl
    off_mg = off_kr + QK_ROPE
    off_glu = off_mg + mla
    off_cg = off_glu + 2 * c
    off_merge = off_cg + c
    perm = _swap_perm()
    w_u = _cast_rows(w_in, l, 0, f)
    w_fg = _cast_rows(w_in, l, off_fg, f)
    w_q = _cast_rows(w_in, l, off_q, ql)
    w_kv = _cast_rows(w_in, l, off_kv, kvl + LANE)
    n_in = w_in.shape[1]
    w_tail = _cast_rows(w_in, l, off_mg, n_in - off_mg)
    tail = dict(mg=(0, mla), glu=(off_glu - off_mg, 2 * c), cg=(off_cg - off_mg, c),
                merge=(off_merge - off_mg, n_in - off_merge))

    wq3 = w_uq[l].reshape(ql, h, QK_DIM)
    q_nope = wq3[:, :, :QK_NOPE].reshape(ql, h * LANE)
    q_rope = wq3[:, :, QK_NOPE:]
    w_uq_p = jnp.concatenate([q_nope, _pad_lanes(q_rope).reshape(ql, h * LANE),
                              _pad_lanes(q_rope[:, :, perm]).reshape(ql, h * LANE)], axis=1).astype(BF16)
    wkv3 = w_ukv[l].reshape(kvl, h, QK_NOPE + V_DIM)
    w_ukv_p = jnp.concatenate([wkv3[:, :, :QK_NOPE].reshape(kvl, h * LANE),
                               wkv3[:, :, QK_NOPE:].reshape(kvl, h * LANE)], axis=1).astype(BF16)

    def gains(g):
        return (g[None, :QK_NOPE], _pad_lanes(g[None, QK_NOPE:]), _pad_lanes(g[None, QK_NOPE:][:, perm]))

    return dict(w_u=w_u, w_fg=w_fg, w_q=w_q, w_kv=w_kv, w_tail=w_tail, tail=tail,
                w_uq=w_uq_p, w_ukv=w_ukv_p, qg=gains(q_norm[l]), kg=gains(k_norm[l]),
                w_fnet=w_fnet[l].astype(BF16), w_pw2=w_pw2[l].astype(BF16),
                w_br_f=w_br_f[l].astype(BF16), w_br_m=w_br_m[l].astype(BF16), w_br_c=w_br_c[l].astype(BF16),
                w_out=w_out[l].astype(BF16))


def _kv_stream(h, w, kv_a_norm, cos, sin, n_heads, kv_lora):
    gn, gr, gs = w["kg"]
    return _kvprep(h, w["w_kv"], kv_a_norm, w["w_ukv"], gn, gr, gs, cos, sin, n_heads, kv_lora)


def _full_stream(x, h, kv_own, kv_ctx, gate, w, p, cos, sin, dims):
    d, f, ql, kvl, n_heads, c = dims
    bsz, s, _ = x.shape
    m = bsz * s
    h2 = h.reshape(m, d)
    wt, tail = w["w_tail"], w["tail"]
    nt = dict(b_t=True)
    zu = _mm(h2, w["w_u"], tn=1024, out_dtype=BF16, name="in_proj_u", **nt)
    gate_f = _mm(h2, w["w_fg"], tn=1024, out_dtype=BF16, epi=_silu, name="in_proj_gate_f", **nt)
    zq = _mm(h2, w["w_q"], tn=ql, out_dtype=F32, name="in_proj_q", **nt)
    gate_m = _mm(h2, wt, b_cols=tail["mg"], tn=1024, out_dtype=BF16, epi=_silu, name="in_proj_gate_m", **nt)
    gate_c = _mm(h2, wt, b_cols=tail["cg"], tn=1024, out_dtype=BF16, epi=_silu, name="in_proj_gate_c", **nt)
    gmerge = _mm(h2, wt, b_cols=tail["merge"], tn=1024, out_dtype=BF16, epi=_sigmoid, name="in_proj_merge",
                 **nt)

    gn, gr, gs = w["qg"]
    q = _qprep(zq.reshape(bsz, s, ql), p["q_a_norm"], w["w_uq"], gn, gr, gs, cos, sin, n_heads)
    sources = ([kv_ctx] if kv_ctx is not None else []) + [kv_own]
    y_m = _attention(q, sources, gate_m.reshape(bsz, s, -1), 0, n_heads)
    y_f = _fourier_branch(zu.reshape(bsz, s, f), w["w_fnet"], gate_f.reshape(bsz, s, f), 0)
    y_c = _conv_branch(h, wt, tail["glu"], p["conv_w"], p["conv_b"], p["cln_g"], p["cln_b"],
                       w["w_pw2"], gate_c.reshape(bsz, s, c))
    merged = _merge(y_f.reshape(m, f), y_m.reshape(m, -1), y_c.reshape(m, c),
                    w["w_br_f"], w["w_br_m"], w["w_br_c"], gmerge)
    out = _mm(merged, w["w_out"], out_dtype=F32, rows_per_batch=s,
              epi=lambda acc, xv, gv: xv + gv * acc,
              extras=(("mn", x.reshape(m, d)), ("bn", gate)), name="out_proj")
    return out.reshape(bsz, s, d)


def kernel(x, c, ctx, c_ctx, norm_g, w_ada, b_ada, w_in, q_a_norm, w_uq, kv_a_norm, w_ukv, q_norm, k_norm,
           w_fnet, conv_w, conv_b, cln_g, cln_b, w_pw2, w_br_f, w_br_m, w_br_c, w_out):
    bsz, n_tok, d = x.shape
    n_ctx = ctx.shape[1]
    depth = w_in.shape[0]
    f = w_fnet.shape[1]
    ql = q_a_norm.shape[1]
    kvl = kv_a_norm.shape[1]
    n_heads = w_uq.shape[2] // QK_DIM
    cdim = conv_b.shape[1]
    dims = (d, f, ql, kvl, n_heads, cdim)

    cos_l, sin_l = _rope_tables(n_tok)
    cos_c = jnp.ones((n_ctx, LANE), F32)
    sin_c = jnp.zeros((n_ctx, LANE), F32)

    n_rows = -(-(bsz + 1) // 8) * 8
    cond = jnp.concatenate([c, c_ctx[None, :], jnp.zeros((n_rows - bsz - 1, d), F32)], axis=0)

    w_in_t = jnp.swapaxes(w_in, 1, 2)

    xl, xc = x, ctx
    for l in range(depth):
        last = l == depth - 1
        w = _prep_layer(l, dims, w_in_t, w_uq, w_ukv, q_norm, k_norm, w_fnet, w_pw2, w_br_f, w_br_m, w_br_c, w_out)
        p = dict(q_a_norm=q_a_norm[l][None], conv_w=conv_w[l], conv_b=conv_b[l][None],
                 cln_g=cln_g[l][None], cln_b=cln_b[l][None])
        mod = _mm(cond, w_ada, b_layer=l, out_dtype=F32, tm=n_rows, tn=1024, tk=2048, a_act=_silu,
                  epi=lambda acc, bias: acc + bias, extras=(("n", b_ada[l][None]),), name="adaln")
        shift_l, scale_l, gate_l = (mod[:bsz, i * d:(i + 1) * d][:, None, :] for i in range(3))
        shift_c, scale_c, gate_c = (jnp.broadcast_to(mod[bsz, i * d:(i + 1) * d][None, None, :], (bsz, 1, d))
                                    for i in range(3))
        g = norm_g[l][None]
        hl = _norm_mod(xl, g, scale_l, shift_l)
        hc = _norm_mod(xc, g, scale_c, shift_c)
        kv_c = _kv_stream(hc, w, kv_a_norm[l][None], cos_c, sin_c, n_heads, kvl)
        kv_l = _kv_stream(hl, w, kv_a_norm[l][None], cos_l, sin_l, n_heads, kvl)
        new_xl = _full_stream(xl, hl, kv_l, kv_c, gate_l, w, p, cos_l, sin_l, dims)
        if not last:
            xc = _full_stream(xc, hc, kv_c, None, gate_c, w, p, cos_c, sin_c, dims)
        xl = new_xl
    return xl
```

```python
import functools
import math

import numpy as np
import jax
import jax.numpy as jnp
from jax import lax
from jax.experimental import pallas as pl
from jax.experimental.pallas import tpu as pltpu

F32 = jnp.float32
BF16 = jnp.bfloat16

EPS = 1e-6
QK_NOPE = 128
QK_ROPE = 64
V_DIM = 128
QK_DIM = QK_NOPE + QK_ROPE
HEAD_PAD = 256
LANE = 128
SUBLANE = 8
GRID_W = 64
CONV_W = 31
CONV_HALO = 16
ROPE_BASE = 10000.0
N_BRANCH = 3
ATTN_TQ = 1024
ATTN_TK = 2048
VMEM_LIMIT = 52 * 1024 * 1024


def _cparams(sem):
    return pltpu.CompilerParams(dimension_semantics=sem, vmem_limit_bytes=VMEM_LIMIT)


def _tile(n, pref, align=8):
    if n <= pref:
        return n
    t = (pref // align) * align
    while t >= align:
        if n % t == 0:
            return t
        t -= align
    return n


def _sigmoid(v):
    return 0.5 * jnp.tanh(0.5 * v) + 0.5


def _silu(v):
    return v * _sigmoid(v)


def _dot(a, b):
    return jnp.dot(a, b, preferred_element_type=F32)


MM_SUB = 256


def _mm_body(*refs, nk, n_extra, epi, a_act, b_t):
    a_ref, b_ref = refs[0], refs[1]
    extra = refs[2:2 + n_extra]
    o_ref = refs[2 + n_extra]
    a = a_ref[...]
    if a_act is not None:
        a = a_act(a.astype(F32))
    a = a.astype(BF16)
    tn = o_ref.shape[-1]

    def prod(c0, width):
        if b_t:
            return lax.dot_general(a, b_ref[c0:c0 + width, :].astype(BF16), (((1,), (1,)), ((), ())),
                                   preferred_element_type=F32)
        return _dot(a, b_ref[:, c0:c0 + width].astype(BF16))

    if nk == 1:
        sub = MM_SUB if tn % MM_SUB == 0 else tn
        for c0 in range(0, tn, sub):
            p = prod(c0, sub)
            o_ref[:, c0:c0 + sub] = epi(p, *[e[:, c0:c0 + sub] for e in extra]).astype(o_ref.dtype)
    else:
        p = prod(0, tn)
        acc_ref = refs[3 + n_extra]
        k = pl.program_id(2)

        @pl.when(k == 0)
        def _():
            acc_ref[...] = p

        @pl.when(k > 0)
        def _():
            acc_ref[...] += p

        @pl.when(k == nk - 1)
        def _():
            o_ref[...] = epi(acc_ref[...], *[e[...] for e in extra]).astype(o_ref.dtype)


def _mm(a, b, *, out_dtype, tm=1024, tn=512, tk=4096, epi=None, extras=(), a_act=None,
        rows_per_batch=None, b_cols=None, b_layer=None, b_t=False, name="mm"):
    m, kdim = a.shape
    col0, n = (0, b.shape[-2 if b_t else -1]) if b_cols is None else b_cols
    tm = _tile(m, tm)
    if rows_per_batch is not None:
        tm = _tile(rows_per_batch, tm)
    tn = _tile(n, tn, LANE)
    while col0 % tn:
        tn = _tile(n, tn - LANE, LANE)
    tk = _tile(kdim, tk, LANE)
    nk = kdim // tk
    jb = col0 // tn
    if epi is None:
        epi = lambda p: p
    lead = (None,) if b.ndim == 3 else ()
    lidx = (b_layer,) if b.ndim == 3 else ()
    if b_t:
        b_spec = pl.BlockSpec(lead + (tn, tk), lambda i, j, k: lidx + (j + jb, k))
    else:
        b_spec = pl.BlockSpec(lead + (tk, tn), lambda i, j, k: lidx + (k, j + jb))
    in_specs = [pl.BlockSpec((tm, tk), lambda i, j, k: (i, k)), b_spec]
    args = [a, b]
    for kind, arr in extras:
        if kind == "mn":
            in_specs.append(pl.BlockSpec((tm, tn), lambda i, j, k: (i, j)))
        elif kind == "n":
            in_specs.append(pl.BlockSpec((1, tn), lambda i, j, k: (0, j)))
        else:
            bpt = rows_per_batch // tm
            in_specs.append(pl.BlockSpec((None, 1, tn), lambda i, j, k, bpt=bpt: (i // bpt, 0, j)))
        args.append(arr)
    scratch = [pltpu.VMEM((tm, tn), F32)] if nk > 1 else []
    return pl.pallas_call(
        functools.partial(_mm_body, nk=nk, n_extra=len(extras), epi=epi, a_act=a_act, b_t=b_t),
        grid=(m // tm, n // tn, nk),
        in_specs=in_specs,
        out_specs=pl.BlockSpec((tm, tn), lambda i, j, k: (i, j)),
        out_shape=jax.ShapeDtypeStruct((m, n), out_dtype),
        scratch_shapes=scratch,
        compiler_params=_cparams(("parallel", "parallel", "arbitrary")),
        name=name,
    )(*args)


def _cast_rows_body(w_ref, o_ref):
    o_ref[...] = w_ref[...].astype(o_ref.dtype)


def _cast_rows_offset_body(main_ref, next_ref, o_ref):
    half = next_ref.shape[0]
    rb = main_ref.shape[0]
    o_ref[0:rb - half, :] = main_ref[half:rb, :].astype(o_ref.dtype)
    o_ref[rb - half:rb, :] = next_ref[...].astype(o_ref.dtype)


def _cast_rows(w, l, start, n_rows):
    kdim = w.shape[2]
    half = LANE // 2
    if start % LANE == half and n_rows % LANE == 0:
        base = start - half
        rb = 2 * LANE if base % (2 * LANE) == 0 and n_rows % (2 * LANE) == 0 else LANE
        jb, r = base // rb, rb // half
        return pl.pallas_call(
            _cast_rows_offset_body,
            grid=(n_rows // rb,),
            in_specs=[pl.BlockSpec((None, rb, kdim), lambda j: (l, jb + j, 0)),
                      pl.BlockSpec((None, half, kdim), lambda j: (l, (jb + j + 1) * r, 0))],
            out_specs=pl.BlockSpec((rb, kdim), lambda j: (j, 0)),
            out_shape=jax.ShapeDtypeStruct((n_rows, kdim), BF16),
            compiler_params=_cparams(("parallel",)),
            name="cast_rows_offset",
        )(w, w)
    rb = 512
    while start % rb or n_rows % rb:
        rb //= 2
    jb = start // rb
    return pl.pallas_call(
        _cast_rows_body,
        grid=(n_rows // rb,),
        in_specs=[pl.BlockSpec((None, rb, kdim), lambda j: (l, jb + j, 0))],
        out_specs=pl.BlockSpec((rb, kdim), lambda j: (j, 0)),
        out_shape=jax.ShapeDtypeStruct((n_rows, kdim), BF16),
        compiler_params=_cparams(("parallel",)),
        name="cast_rows",
    )(w)


def _norm_body(x_ref, g_ref, scale_ref, shift_ref, o_ref):
    x = x_ref[...]
    r = lax.rsqrt(jnp.mean(x * x, axis=-1, keepdims=True) + EPS)
    y = x * r * g_ref[...]
    o_ref[...] = (y * (1.0 + scale_ref[...]) + shift_ref[...]).astype(o_ref.dtype)


def _norm_mod(x, g, scale, shift):
    b, s, d = x.shape
    tr = _tile(s, 256)
    return pl.pallas_call(
        _norm_body,
        grid=(b, s // tr),
        in_specs=[pl.BlockSpec((None, tr, d), lambda bi, i: (bi, i, 0)),
                  pl.BlockSpec((1, d), lambda bi, i: (0, 0)),
                  pl.BlockSpec((None, 1, d), lambda bi, i: (bi, 0, 0)),
                  pl.BlockSpec((None, 1, d), lambda bi, i: (bi, 0, 0))],
        out_specs=pl.BlockSpec((None, tr, d), lambda bi, i: (bi, i, 0)),
        out_shape=jax.ShapeDtypeStruct((b, s, d), BF16),
        compiler_params=_cparams(("parallel", "parallel")),
        name="norm_mod",
    )(x, g, scale, shift)


def _swap_quarters(v, even_quarter):
    quarter = QK_ROPE // 4
    return jnp.where(even_quarter, pltpu.roll(v, LANE - quarter, axis=1), pltpu.roll(v, quarter, axis=1))


_NT = (((1,), (1,)), ((), ()))


def _qprep_body(z_ref, an_ref, w_ref, gn_ref, gr_ref, gs_ref, cos_ref, sin_ref, q_ref, *, n_heads, scale):
    x = z_ref[...]
    r = lax.rsqrt(jnp.mean(x * x, axis=-1, keepdims=True) + EPS)
    xn = (x * r * an_ref[...]).astype(BF16)
    acc = _dot(xn, w_ref[...])
    cos = cos_ref[...] * gr_ref[...]
    sin = sin_ref[...] * gs_ref[...]
    gn = gn_ref[...]
    hw = n_heads * LANE
    for h in range(n_heads):
        nope = acc[:, h * LANE:(h + 1) * LANE]
        rp = acc[:, hw + h * LANE: hw + (h + 1) * LANE]
        rs = acc[:, 2 * hw + h * LANE: 2 * hw + (h + 1) * LANE]
        ss = jnp.sum(nope * nope + rp * rp, axis=-1, keepdims=True)
        inv = lax.rsqrt(ss * (1.0 / QK_DIM) + EPS) * scale
        q_ref[h, :, 0:LANE] = (nope * inv * gn).astype(q_ref.dtype)
        q_ref[h, :, LANE:HEAD_PAD] = ((rp * cos + rs * sin) * inv).astype(q_ref.dtype)


def _qprep(zq, a_norm, w_p, gn, gr, gs, cos, sin, n_heads):
    b, s, ql = zq.shape
    tr = _tile(s, 512)
    nw = w_p.shape[1]
    return pl.pallas_call(
        functools.partial(_qprep_body, n_heads=n_heads, scale=QK_DIM ** -0.5),
        grid=(b, s // tr),
        in_specs=[pl.BlockSpec((None, tr, ql), lambda bi, i: (bi, i, 0)),
                  pl.BlockSpec((1, ql), lambda bi, i: (0, 0)),
                  pl.BlockSpec((ql, nw), lambda bi, i: (0, 0), pipeline_mode=pl.Buffered(1)),
                  pl.BlockSpec((1, LANE), lambda bi, i: (0, 0)),
                  pl.BlockSpec((1, LANE), lambda bi, i: (0, 0)),
                  pl.BlockSpec((1, LANE), lambda bi, i: (0, 0)),
                  pl.BlockSpec((tr, LANE), lambda bi, i: (i, 0)),
                  pl.BlockSpec((tr, LANE), lambda bi, i: (i, 0))],
        out_specs=pl.BlockSpec((None, n_heads, tr, HEAD_PAD), lambda bi, i: (bi, 0, i, 0)),
        out_shape=jax.ShapeDtypeStruct((b, n_heads, s, HEAD_PAD), BF16),
        compiler_params=_cparams(("parallel", "parallel")),
        name="q_prep",
    )(zq, a_norm, w_p, gn, gr, gs, cos, sin)


def _kvprep_body(h_ref, wkv_ref, an_ref, w_ref, gn_ref, gr_ref, gs_ref, cos_ref, sin_ref, k_ref, v_ref, *,
                 n_heads, kv_lora):
    z = lax.dot_general(h_ref[...], wkv_ref[...], _NT, preferred_element_type=F32)
    x = z[:, 0:kv_lora]
    tr = x.shape[0]
    lane = lax.broadcasted_iota(jnp.int32, (tr, LANE), 1)
    kr = jnp.where(lane < QK_ROPE, z[:, kv_lora:kv_lora + LANE], 0.0)
    krs = _swap_quarters(kr, (lane // (QK_ROPE // 4)) % 2 == 0)
    r = lax.rsqrt(jnp.mean(x * x, axis=-1, keepdims=True) + EPS)
    xn = (x * r * an_ref[...]).astype(BF16)
    acc = _dot(xn, w_ref[...])
    ss_r = jnp.sum(kr * kr, axis=-1, keepdims=True)
    rope = kr * (cos_ref[...] * gr_ref[...]) + krs * (sin_ref[...] * gs_ref[...])
    gn = gn_ref[...]
    hw = n_heads * LANE
    ones_col = (lane == 0).astype(v_ref.dtype)
    for h in range(n_heads):
        kn = acc[:, h * LANE:(h + 1) * LANE]
        ss = jnp.sum(kn * kn, axis=-1, keepdims=True) + ss_r
        inv = lax.rsqrt(ss * (1.0 / QK_DIM) + EPS)
        k_ref[h, :, 0:LANE] = (kn * inv * gn).astype(k_ref.dtype)
        k_ref[h, :, LANE:HEAD_PAD] = (rope * inv).astype(k_ref.dtype)
        v_ref[h, :, 0:LANE] = acc[:, hw + h * LANE: hw + (h + 1) * LANE].astype(v_ref.dtype)
        v_ref[h, :, LANE:HEAD_PAD] = ones_col


def _kvprep(h, w_kv, a_norm, w_p, gn, gr, gs, cos, sin, n_heads, kv_lora):
    b, s, d = h.shape
    zw = w_kv.shape[0]
    tr = _tile(s, 256)
    nw = w_p.shape[1]
    out = jax.ShapeDtypeStruct((b, n_heads, s, HEAD_PAD), BF16)
    ospec = pl.BlockSpec((None, n_heads, tr, HEAD_PAD), lambda bi, i: (bi, 0, i, 0))
    once = dict(pipeline_mode=pl.Buffered(1))
    return pl.pallas_call(
        functools.partial(_kvprep_body, n_heads=n_heads, kv_lora=kv_lora),
        grid=(b, s // tr),
        in_specs=[pl.BlockSpec((None, tr, d), lambda bi, i: (bi, i, 0)),
                  pl.BlockSpec((zw, d), lambda bi, i: (0, 0), **once),
                  pl.BlockSpec((1, kv_lora), lambda bi, i: (0, 0)),
                  pl.BlockSpec((kv_lora, nw), lambda bi, i: (0, 0), **once),
                  pl.BlockSpec((1, LANE), lambda bi, i: (0, 0)),
                  pl.BlockSpec((1, LANE), lambda bi, i: (0, 0)),
                  pl.BlockSpec((1, LANE), lambda bi, i: (0, 0)),
                  pl.BlockSpec((tr, LANE), lambda bi, i: (i, 0)),
                  pl.BlockSpec((tr, LANE), lambda bi, i: (i, 0))],
        out_specs=[ospec, ospec],
        out_shape=[out, out],
        compiler_params=_cparams(("parallel", "parallel")),
        name="kv_prep",
    )(h, w_kv, a_norm, w_p, gn, gr, gs, cos, sin)


def _attn_body(*refs, n_src, chunks):
    q_ref = refs[0]
    kv = refs[1:1 + 2 * n_src]
    gate_ref = refs[1 + 2 * n_src]
    o_ref = refs[2 + 2 * n_src]
    q = q_ref[...]
    tq = q.shape[0]
    plan = []
    for si in range(n_src):
        tk, n_chunks = chunks[si]
        plan += [(kv[2 * si], kv[2 * si + 1], ci * tk, tk) for ci in range(n_chunks)]

    def scores(i):
        k_ref, _, start, tk = plan[i]
        return lax.dot_general(q, k_ref[start:start + tk, :], (((1,), (1,)), ((), ())),
                               preferred_element_type=F32)

    m = jnp.full((tq, LANE), -jnp.inf, F32)
    acc = jnp.zeros((tq, HEAD_PAD), F32)
    s_next = scores(0)
    for i in range(len(plan)):
        s = s_next
        if i + 1 < len(plan):
            s_next = scores(i + 1)
        _, v_ref, start, tk = plan[i]
        m_new = jnp.maximum(m, jnp.max(s, axis=-1, keepdims=True))
        p = jnp.exp((s - jnp.tile(m_new, (1, tk // LANE))).astype(BF16))
        alpha = jnp.exp(m - m_new)
        acc = acc * jnp.tile(alpha, (1, HEAD_PAD // LANE)) + _dot(p, v_ref[start:start + tk, :])
        m = m_new
    out = acc[:, 0:V_DIM] / acc[:, V_DIM:V_DIM + 1]
    o_ref[...] = (out * gate_ref[...].astype(F32)).astype(o_ref.dtype)


def _attention(q, sources, gates, gate_col, n_heads):
    b, _, s, _ = q.shape
    tq = _tile(s, ATTN_TQ)
    in_specs = [pl.BlockSpec((None, None, tq, HEAD_PAD), lambda bi, h, i: (bi, h, i, 0))]
    args = [q]
    chunks = []
    for k, v in sources:
        sk = k.shape[2]
        tk = _tile(sk, ATTN_TK)
        chunks.append((tk, sk // tk))
        spec = pl.BlockSpec((None, None, sk, HEAD_PAD), lambda bi, h, i: (bi, h, 0, 0))
        in_specs += [spec, spec]
        args += [k, v]
    gb = gate_col // V_DIM
    in_specs.append(pl.BlockSpec((None, tq, V_DIM), lambda bi, h, i: (bi, i, gb + h)))
    args.append(gates)
    return pl.pallas_call(
        functools.partial(_attn_body, n_src=len(sources), chunks=tuple(chunks)),
        grid=(b, n_heads, s // tq),
        in_specs=in_specs,
        out_specs=pl.BlockSpec((None, tq, V_DIM), lambda bi, h, i: (bi, i, h)),
        out_shape=jax.ShapeDtypeStruct((b, s, n_heads * V_DIM), BF16),
        compiler_params=_cparams(("parallel", "parallel", "arbitrary")),
        name="attention",
    )(*args)


def _fft_factors(s):
    n1 = 128 if s % 128 == 0 and s // 128 >= 16 else 16
    return n1, s // n1


@functools.lru_cache(maxsize=None)
def _fft_tables(s, c):
    n1, n2 = _fft_factors(s)
    k1 = np.arange(n1, dtype=np.float64)
    a1 = 2.0 * np.pi * np.outer(k1, k1) / n1
    t1 = np.concatenate([np.cos(a1), -np.sin(a1)], axis=0)
    at = 2.0 * np.pi * np.outer(np.arange(n2, dtype=np.float64), k1) / s
    twr = np.cos(at)[:, :, None]
    twi = (-np.sin(at))[:, :, None]
    k2 = np.arange(n2, dtype=np.float64)
    a3 = 2.0 * np.pi * np.outer(k2, k2) / n2
    c3, s3 = np.cos(a3), np.sin(a3)
    t3 = np.block([[c3, s3], [-s3, c3]])
    kc = np.arange(c, dtype=np.float64)
    ac = 2.0 * np.pi * np.outer(kc, kc) / c
    tc = np.stack([np.cos(ac), np.sin(ac)], axis=0) / math.sqrt(float(s) * float(c))

    bf = lambda t: t.astype(np.float32).astype(BF16)
    return dict(t1=bf(t1), twr=twr.astype(np.float32), twi=twi.astype(np.float32), t3=bf(t3), tc=bf(tc))


def _fft1_body(u_ref, t_ref, twr_ref, twi_ref, y_ref, s_ref, *, n1, jt):
    c = s_ref.shape[-1]
    for j in range(jt):
        y = _dot(t_ref[...], u_ref[:, j * c:(j + 1) * c])
        yr, yi = y[0:n1], y[n1:2 * n1]
        tr, ti = twr_ref[j], twi_ref[j]
        s_ref[0, :, j, :] = yr * tr - yi * ti
        s_ref[1, :, j, :] = yr * ti + yi * tr
    for k in range(n1):
        y_ref[0, :, k * c:(k + 1) * c] = s_ref[0, k].astype(y_ref.dtype)
        y_ref[1, :, k * c:(k + 1) * c] = s_ref[1, k].astype(y_ref.dtype)


def _fft3_body(y_ref, t_ref, x_ref):
    x_ref[...] = _dot(t_ref[...], y_ref[...]).astype(x_ref.dtype)


def _fftc_body(x_ref, t_ref, wf_ref, gate_ref, o_ref, scr_ref, *, n2, kt):
    c = wf_ref.shape[0]
    xr = jnp.concatenate([x_ref[0:n2, j * c:(j + 1) * c] for j in range(kt)], axis=0)
    xi = jnp.concatenate([x_ref[n2:2 * n2, j * c:(j + 1) * c] for j in range(kt)], axis=0)
    z = _dot(xr, t_ref[0]) + _dot(xi, t_ref[1])
    y = _dot(z.astype(BF16), wf_ref[...])
    for j in range(kt):
        scr_ref[:, j, :] = y[j * n2:(j + 1) * n2, :]
    o_ref[...] = (scr_ref[...] * gate_ref[...].astype(F32)).astype(o_ref.dtype)


def _fourier_branch(u, w_fnet, gates, gate_col):
    b, s, c = u.shape
    n1, n2 = _fft_factors(s)
    tabs = _fft_tables(s, c)
    t1, t3, tc = (jnp.asarray(tabs[k]) for k in ("t1", "t3", "tc"))
    twr, twi = jnp.asarray(tabs["twr"]), jnp.asarray(tabs["twi"])

    const2 = lambda bi, j: (0, 0)
    jt = min(16, n2)
    y = pl.pallas_call(
        functools.partial(_fft1_body, n1=n1, jt=jt),
        grid=(b, n2 // jt),
        in_specs=[pl.BlockSpec((None, n1, jt * c), lambda bi, j: (bi, 0, j)),
                  pl.BlockSpec((2 * n1, n1), const2),
                  pl.BlockSpec((jt, n1, 1), lambda bi, j: (j, 0, 0)),
                  pl.BlockSpec((jt, n1, 1), lambda bi, j: (j, 0, 0))],
        out_specs=pl.BlockSpec((None, 2, jt, n1 * c), lambda bi, j: (bi, 0, j, 0)),
        out_shape=jax.ShapeDtypeStruct((b, 2, n2, n1 * c), BF16),
        scratch_shapes=[pltpu.VMEM((2, n1, jt, c), F32)],
        compiler_params=_cparams(("parallel", "parallel")),
        name="fft_stage1",
    )(u.reshape(b, n1, n2 * c), t1, twr, twi)

    cols = n1 * c
    tn = _tile(cols, 8192, LANE)
    x = pl.pallas_call(
        _fft3_body,
        grid=(b, cols // tn),
        in_specs=[pl.BlockSpec((None, 2 * n2, tn), lambda bi, j: (bi, 0, j)),
                  pl.BlockSpec((2 * n2, 2 * n2), const2)],
        out_specs=pl.BlockSpec((None, 2 * n2, tn), lambda bi, j: (bi, 0, j)),
        out_shape=jax.ShapeDtypeStruct((b, 2 * n2, cols), BF16),
        compiler_params=_cparams(("parallel", "parallel")),
        name="fft_stage2",
    )(y.reshape(b, 2 * n2, cols), t3)

    assert gate_col == 0 and gates.shape[-1] == c
    kt = min(16, n1)
    once = dict(pipeline_mode=pl.Buffered(1))
    seq_spec = pl.BlockSpec((None, n2, kt, c), lambda bi, j: (bi, 0, j, 0))
    yf = pl.pallas_call(
        functools.partial(_fftc_body, n2=n2, kt=kt),
        grid=(b, n1 // kt),
        in_specs=[pl.BlockSpec((None, 2 * n2, kt * c), lambda bi, j: (bi, 0, j)),
                  pl.BlockSpec((2, c, c), lambda bi, j: (0, 0, 0), **once),
                  pl.BlockSpec((c, c), lambda bi, j: (0, 0), **once),
                  seq_spec],
        out_specs=seq_spec,
        out_shape=jax.ShapeDtypeStruct((b, n2, n1, c), BF16),
        scratch_shapes=[pltpu.VMEM((n2, kt, c), F32)],
        compiler_params=_cparams(("parallel", "parallel")),
        name="fft_channel",
    )(x, tc, w_fnet, gates.reshape(b, n2, n1, c))
    return yf.reshape(b, s, c)


CONV_GROUP = 256


def _conv_body(hl_ref, hc_ref, hr_ref, wa_ref, wg_ref, cw_ref, cb_ref, lg_ref, lb_ref, w_ref, gate_ref, o_ref,
               ext_ref, sh_ref, acc_ref, *, c, ts, n_tiles):
    i = pl.program_id(1)
    h_ext = jnp.concatenate([hl_ref[...], hc_ref[...], hr_ref[...]], axis=0)
    row = lax.broadcasted_iota(jnp.int32, (ts + 2 * CONV_HALO, 1), 0)
    inside = ((row >= CONV_HALO) | (i > 0)) & ((row < CONV_HALO + ts) | (i < n_tiles - 1))
    span = sh_ref.shape[1]
    off = CONV_HALO - CONV_W // 2
    nt = (((1,), (1,)), ((), ()))
    group = min(CONV_GROUP, c)
    for c0 in range(0, c, group):
        cs = slice(c0, c0 + group)
        za = lax.dot_general(h_ext, wa_ref[cs, :], nt, preferred_element_type=F32)
        zg = lax.dot_general(h_ext, wg_ref[cs, :], nt, preferred_element_type=F32)
        ext_ref[:, cs] = jnp.where(inside, za * _sigmoid(zg), 0.0)
        for r in range(1, SUBLANE):
            sh_ref[r - 1, :, cs] = ext_ref[r:r + span, cs]
        acc = jnp.zeros((ts, group), F32) + cb_ref[:, cs]
        for j in range(CONV_W):
            r, base = (off + j) % SUBLANE, (off + j) // SUBLANE * SUBLANE
            src = ext_ref[base:base + ts, cs] if r == 0 else sh_ref[r - 1, base:base + ts, cs]
            acc = acc + src * cw_ref[j:j + 1, cs]
        acc_ref[:, cs] = acc
    acc = acc_ref[...]
    mu = jnp.mean(acc, axis=-1, keepdims=True)
    d = acc - mu
    var = jnp.mean(d * d, axis=-1, keepdims=True)
    y = _silu(d * lax.rsqrt(var + EPS) * lg_ref[...] + lb_ref[...])
    out = _dot(y.astype(BF16), w_ref[...])
    o_ref[...] = (out * gate_ref[...].astype(F32)).astype(o_ref.dtype)


def _conv_branch(h, w_t, glu_rows, conv_w, conv_b, cln_g, cln_b, w_pw2, gates):
    b, s, d = h.shape
    start, c2 = glu_rows
    c = c2 // 2
    ts = _tile(s, 256, CONV_HALO)
    n_tiles = s // ts
    hb = ts // CONV_HALO
    n_hblk = s // CONV_HALO
    row = lambda bi, i: (0, 0)
    once = dict(pipeline_mode=pl.Buffered(1))
    ja = start // c
    return pl.pallas_call(
        functools.partial(_conv_body, c=c, ts=ts, n_tiles=n_tiles),
        grid=(b, n_tiles),
        in_specs=[pl.BlockSpec((None, CONV_HALO, d), lambda bi, i: (bi, jnp.maximum(i * hb - 1, 0), 0)),
                  pl.BlockSpec((None, ts, d), lambda bi, i: (bi, i, 0)),
                  pl.BlockSpec((None, CONV_HALO, d),
                               lambda bi, i: (bi, jnp.minimum((i + 1) * hb, n_hblk - 1), 0)),
                  pl.BlockSpec((c, d), lambda bi, i: (ja, 0), **once),
                  pl.BlockSpec((c, d), lambda bi, i: (ja + 1, 0), **once),
                  pl.BlockSpec((CONV_W, c), row),
                  pl.BlockSpec((1, c), row),
                  pl.BlockSpec((1, c), row),
                  pl.BlockSpec((1, c), row),
                  pl.BlockSpec((c, c), row, **once),
                  pl.BlockSpec((None, ts, c), lambda bi, i: (bi, i, 0))],
        out_specs=pl.BlockSpec((None, ts, c), lambda bi, i: (bi, i, 0)),
        out_shape=jax.ShapeDtypeStruct((b, s, c), BF16),
        scratch_shapes=[pltpu.VMEM((ts + 2 * CONV_HALO, c), F32),
                        pltpu.VMEM((SUBLANE - 1, ts + 2 * CONV_HALO - SUBLANE, c), F32),
                        pltpu.VMEM((ts, c), F32)],
        compiler_params=_cparams(("parallel", "arbitrary")),
        name="conv_branch",
    )(h, h, h, w_t, w_t, conv_w, conv_b, cln_g, cln_b, w_pw2, gates)


def _merge_body(yf_ref, ym_ref, yc_ref, wf_ref, wm_ref, wc_ref, gf_ref, gm_ref, gc_ref, o_ref):
    yf, ym, yc = yf_ref[...], ym_ref[...], yc_ref[...]
    tn = o_ref.shape[-1]
    sub = MM_SUB if tn % MM_SUB == 0 else tn
    for c0 in range(0, tn, sub):
        cs = slice(c0, c0 + sub)
        acc = gf_ref[:, cs].astype(F32) * _dot(yf, wf_ref[:, cs])
        acc = acc + gm_ref[:, cs].astype(F32) * _dot(ym, wm_ref[:, cs])
        acc = acc + gc_ref[:, cs].astype(F32) * _dot(yc, wc_ref[:, cs])
        o_ref[:, cs] = acc.astype(o_ref.dtype)


def _merge(yf, ym, yc, wf, wm, wc, g):
    m = yf.shape[0]
    d = wf.shape[1]
    tm = _tile(m, 512)
    tn = _tile(d, 1024, LANE)
    nj = d // tn
    yspec = lambda arr: pl.BlockSpec((tm, arr.shape[1]), lambda j, i: (i, 0))
    wspec = lambda arr: pl.BlockSpec((arr.shape[0], tn), lambda j, i: (0, j))
    gspec = lambda t: pl.BlockSpec((tm, tn), lambda j, i, t=t: (i, t * nj + j))
    return pl.pallas_call(
        _merge_body,
        grid=(nj, m // tm),
        in_specs=[yspec(yf), yspec(ym), yspec(yc), wspec(wf), wspec(wm), wspec(wc),
                  gspec(0), gspec(1), gspec(2)],
        out_specs=pl.BlockSpec((tm, tn), lambda j, i: (i, j)),
        out_shape=jax.ShapeDtypeStruct((m, d), BF16),
        compiler_params=_cparams(("parallel", "parallel")),
        name="merge",
    )(yf, ym, yc, wf, wm, wc, g, g, g)


def _rope_tables(n_tok):
    half = QK_ROPE // 4
    inv = ROPE_BASE ** (-jnp.arange(half, dtype=F32) / half)
    t = jnp.arange(n_tok, dtype=jnp.int32)
    rows = (t // GRID_W).astype(F32)[:, None] * inv[None, :]
    cols = (t % GRID_W).astype(F32)[:, None] * inv[None, :]
    cr, sr, cc, sc = jnp.cos(rows), jnp.sin(rows), jnp.cos(cols), jnp.sin(cols)
    pad1 = jnp.ones((n_tok, LANE - QK_ROPE), F32)
    pad0 = jnp.zeros((n_tok, LANE - QK_ROPE), F32)
    cos = jnp.concatenate([cr, cr, cc, cc, pad1], axis=1)
    sin = jnp.concatenate([-sr, sr, -sc, sc, pad0], axis=1)
    return cos, sin


def _swap_perm():
    q = QK_ROPE // 4
    return np.concatenate([np.arange(q, 2 * q), np.arange(0, q), np.arange(3 * q, 4 * q), np.arange(2 * q, 3 * q)])


def _pad_lanes(w, width=LANE):
    return jnp.pad(w, [(0, 0)] * (w.ndim - 1) + [(0, width - w.shape[-1])])


def _prep_layer(l, dims, w_in, w_uq, w_ukv, q_norm, k_norm, w_fnet, w_pw2, w_br_f, w_br_m, w_br_c, w_out):
    d, f, ql, kvl, h, c = dims
    mla = h * V_DIM
    off_fg = f
    off_q = 2 * f
    off_kv = off_q + ql
    off_kr = off_kv + kvl
    off_mg = off_kr + QK_ROPE
    off_glu = off_mg + mla
    off_cg = off_glu + 2 * c
    off_merge = off_cg + c
    perm = _swap_perm()
    w_u = _cast_rows(w_in, l, 0, f)
    w_fg = _cast_rows(w_in, l, off_fg, f)
    w_q = _cast_rows(w_in, l, off_q, ql)
    w_kv = _cast_rows(w_in, l, off_kv, kvl + LANE)
    n_in = w_in.shape[1]
    w_tail = _cast_rows(w_in, l, off_mg, n_in - off_mg)
    tail = dict(mg=(0, mla), glu=(off_glu - off_mg, 2 * c), cg=(off_cg - off_mg, c),
                merge=(off_merge - off_mg, n_in - off_merge))

    wq3 = w_uq[l].reshape(ql, h, QK_DIM)
    q_nope = wq3[:, :, :QK_NOPE].reshape(ql, h * LANE)
    q_rope = wq3[:, :, QK_NOPE:]
    w_uq_p = jnp.concatenate([q_nope, _pad_lanes(q_rope).reshape(ql, h * LANE),
                              _pad_lanes(q_rope[:, :, perm]).reshape(ql, h * LANE)], axis=1).astype(BF16)
    wkv3 = w_ukv[l].reshape(kvl, h, QK_NOPE + V_DIM)
    w_ukv_p = jnp.concatenate([wkv3[:, :, :QK_NOPE].reshape(kvl, h * LANE),
                               wkv3[:, :, QK_NOPE:].reshape(kvl, h * LANE)], axis=1).astype(BF16)

    def gains(g):
        return (g[None, :QK_NOPE], _pad_lanes(g[None, QK_NOPE:]), _pad_lanes(g[None, QK_NOPE:][:, perm]))

    return dict(w_u=w_u, w_fg=w_fg, w_q=w_q, w_kv=w_kv, w_tail=w_tail, tail=tail,
                w_uq=w_uq_p, w_ukv=w_ukv_p, qg=gains(q_norm[l]), kg=gains(k_norm[l]),
                w_fnet=w_fnet[l].astype(BF16), w_pw2=w_pw2[l].astype(BF16),
                w_br_f=w_br_f[l].astype(BF16), w_br_m=w_br_m[l].astype(BF16), w_br_c=w_br_c[l].astype(BF16),
                w_out=w_out[l].astype(BF16))


def _kv_stream(h, w, kv_a_norm, cos, sin, n_heads, kv_lora):
    gn, gr, gs = w["kg"]
    return _kvprep(h, w["w_kv"], kv_a_norm, w["w_ukv"], gn, gr, gs, cos, sin, n_heads, kv_lora)


def _full_stream(x, h, kv_own, kv_ctx, gate, w, p, cos, sin, dims):
    d, f, ql, kvl, n_heads, c = dims
    bsz, s, _ = x.shape
    m = bsz * s
    h2 = h.reshape(m, d)
    wt, tail = w["w_tail"], w["tail"]
    nt = dict(b_t=True)
    zu = _mm(h2, w["w_u"], tn=1024, out_dtype=BF16, name="in_proj_u", **nt)
    gate_f = _mm(h2, w["w_fg"], tn=1024, out_dtype=BF16, epi=_silu, name="in_proj_gate_f", **nt)
    zq = _mm(h2, w["w_q"], tn=ql, out_dtype=F32, name="in_proj_q", **nt)
    gate_m = _mm(h2, wt, b_cols=tail["mg"], tn=1024, out_dtype=BF16, epi=_silu, name="in_proj_gate_m", **nt)
    gate_c = _mm(h2, wt, b_cols=tail["cg"], tn=1024, out_dtype=BF16, epi=_silu, name="in_proj_gate_c", **nt)
    gmerge = _mm(h2, wt, b_cols=tail["merge"], tn=1024, out_dtype=BF16, epi=_sigmoid, name="in_proj_merge",
                 **nt)

    gn, gr, gs = w["qg"]
    q = _qprep(zq.reshape(bsz, s, ql), p["q_a_norm"], w["w_uq"], gn, gr, gs, cos, sin, n_heads)
    sources = ([kv_ctx] if kv_ctx is not None else []) + [kv_own]
    y_m = _attention(q, sources, gate_m.reshape(bsz, s, -1), 0, n_heads)
    y_f = _fourier_branch(zu.reshape(bsz, s, f), w["w_fnet"], gate_f.reshape(bsz, s, f), 0)
    y_c = _conv_branch(h, wt, tail["glu"], p["conv_w"], p["conv_b"], p["cln_g"], p["cln_b"],
                       w["w_pw2"], gate_c.reshape(bsz, s, c))
    merged = _merge(y_f.reshape(m, f), y_m.reshape(m, -1), y_c.reshape(m, c),
                    w["w_br_f"], w["w_br_m"], w["w_br_c"], gmerge)
    out = _mm(merged, w["w_out"], out_dtype=F32, rows_per_batch=s,
              epi=lambda acc, xv, gv: xv + gv * acc,
              extras=(("mn", x.reshape(m, d)), ("bn", gate)), name="out_proj")
    return out.reshape(bsz, s, d)


def kernel(x, c, ctx, c_ctx, norm_g, w_ada, b_ada, w_in, q_a_norm, w_uq, kv_a_norm, w_ukv, q_norm, k_norm,
           w_fnet, conv_w, conv_b, cln_g, cln_b, w_pw2, w_br_f, w_br_m, w_br_c, w_out):
    bsz, n_tok, d = x.shape
    n_ctx = ctx.shape[1]
    depth = w_in.shape[0]
    f = w_fnet.shape[1]
    ql = q_a_norm.shape[1]
    kvl = kv_a_norm.shape[1]
    n_heads = w_uq.shape[2] // QK_DIM
    cdim = conv_b.shape[1]
    dims = (d, f, ql, kvl, n_heads, cdim)

    cos_l, sin_l = _rope_tables(n_tok)
    cos_c = jnp.ones((n_ctx, LANE), F32)
    sin_c = jnp.zeros((n_ctx, LANE), F32)

    n_rows = -(-(bsz + 1) // 8) * 8
    cond = jnp.concatenate([c, c_ctx[None, :], jnp.zeros((n_rows - bsz - 1, d), F32)], axis=0)

    w_in_t = jnp.swapaxes(w_in, 1, 2)

    xl, xc = x, ctx
    for l in range(depth):
        last = l == depth - 1
        w = _prep_layer(l, dims, w_in_t, w_uq, w_ukv, q_norm, k_norm, w_fnet, w_pw2, w_br_f, w_br_m, w_br_c, w_out)
        p = dict(q_a_norm=q_a_norm[l][None], conv_w=conv_w[l], conv_b=conv_b[l][None],
                 cln_g=cln_g[l][None], cln_b=cln_b[l][None])
        mod = _mm(cond, w_ada, b_layer=l, out_dtype=F32, tm=n_rows, tn=1024, tk=2048, a_act=_silu,
                  epi=lambda acc, bias: acc + bias, extras=(("n", b_ada[l][None]),), name="adaln")
        shift_l, scale_l, gate_l = (mod[:bsz, i * d:(i + 1) * d][:, None, :] for i in range(3))
        shift_c, scale_c, gate_c = (jnp.broadcast_to(mod[bsz, i * d:(i + 1) * d][None, None, :], (bsz, 1, d))
                                    for i in range(3))
        g = norm_g[l][None]
        hl = _norm_mod(xl, g, scale_l, shift_l)
        hc = _norm_mod(xc, g, scale_c, shift_c)
        kv_c = _kv_stream(hc, w, kv_a_norm[l][None], cos_c, sin_c, n_heads, kvl)
        kv_l = _kv_stream(hl, w, kv_a_norm[l][None], cos_l, sin_l, n_heads, kvl)
        new_xl = _full_stream(xl, hl, kv_l, kv_c, gate_l, w, p, cos_l, sin_l, dims)
        if not last:
            xc = _full_stream(xc, hc, kv_c, None, gate_c, w, p, cos_c, sin_c, dims)
        xl = new_xl
    return xl
```

```python
import functools
import math

import numpy as np
import jax
import jax.numpy as jnp
from jax import lax
from jax.experimental import pallas as pl
from jax.experimental.pallas import tpu as pltpu

F32 = jnp.float32
BF16 = jnp.bfloat16

EPS = 1e-6
QK_NOPE = 128
QK_ROPE = 64
V_DIM = 128
QK_DIM = QK_NOPE + QK_ROPE
HEAD_PAD = 256
LANE = 128
SUBLANE = 8
GRID_W = 64
CONV_W = 31
CONV_HALO = 16
ROPE_BASE = 10000.0
N_BRANCH = 3
ATTN_TQ = 1024
ATTN_TK = 2048
VMEM_LIMIT = 52 * 1024 * 1024


def _cparams(sem):
    return pltpu.CompilerParams(dimension_semantics=sem, vmem_limit_bytes=VMEM_LIMIT)


def _tile(n, pref, align=8):
    if n <= pref:
        return n
    t = (pref // align) * align
    while t >= align:
        if n % t == 0:
            return t
        t -= align
    return n


def _sigmoid(v):
    return 0.5 * jnp.tanh(0.5 * v) + 0.5


def _silu(v):
    return v * _sigmoid(v)


def _dot(a, b):
    return jnp.dot(a, b, preferred_element_type=F32)


MM_SUB = 256


def _mm_body(*refs, nk, n_extra, epi, a_act, b_t):
    a_ref, b_ref = refs[0], refs[1]
    extra = refs[2:2 + n_extra]
    o_ref = refs[2 + n_extra]
    a = a_ref[...]
    if a_act is not None:
        a = a_act(a.astype(F32))
    a = a.astype(BF16)
    tn = o_ref.shape[-1]

    def prod(c0, width):
        if b_t:
            return lax.dot_general(a, b_ref[c0:c0 + width, :].astype(BF16), (((1,), (1,)), ((), ())),
                                   preferred_element_type=F32)
        return _dot(a, b_ref[:, c0:c0 + width].astype(BF16))

    if nk == 1:
        sub = MM_SUB if tn % MM_SUB == 0 else tn
        for c0 in range(0, tn, sub):
            p = prod(c0, sub)
            o_ref[:, c0:c0 + sub] = epi(p, *[e[:, c0:c0 + sub] for e in extra]).astype(o_ref.dtype)
    else:
        p = prod(0, tn)
        acc_ref = refs[3 + n_extra]
        k = pl.program_id(2)

        @pl.when(k == 0)
        def _():
            acc_ref[...] = p

        @pl.when(k > 0)
        def _():
            acc_ref[...] += p

        @pl.when(k == nk - 1)
        def _():
            o_ref[...] = epi(acc_ref[...], *[e[...] for e in extra]).astype(o_ref.dtype)


def _mm(a, b, *, out_dtype, tm=1024, tn=512, tk=4096, epi=None, extras=(), a_act=None,
        rows_per_batch=None, b_cols=None, b_layer=None, b_t=False, name="mm"):
    m, kdim = a.shape
    col0, n = (0, b.shape[-2 if b_t else -1]) if b_cols is None else b_cols
    tm = _tile(m, tm)
    if rows_per_batch is not None:
        tm = _tile(rows_per_batch, tm)
    tn = _tile(n, tn, LANE)
    while col0 % tn:
        tn = _tile(n, tn - LANE, LANE)
    tk = _tile(kdim, tk, LANE)
    nk = kdim // tk
    jb = col0 // tn
    if epi is None:
        epi = lambda p: p
    lead = (None,) if b.ndim == 3 else ()
    lidx = (b_layer,) if b.ndim == 3 else ()
    if b_t:
        b_spec = pl.BlockSpec(lead + (tn, tk), lambda i, j, k: lidx + (j + jb, k))
    else:
        b_spec = pl.BlockSpec(lead + (tk, tn), lambda i, j, k: lidx + (k, j + jb))
    in_specs = [pl.BlockSpec((tm, tk), lambda i, j, k: (i, k)), b_spec]
    args = [a, b]
    for kind, arr in extras:
        if kind == "mn":
            in_specs.append(pl.BlockSpec((tm, tn), lambda i, j, k: (i, j)))
        elif kind == "n":
            in_specs.append(pl.BlockSpec((1, tn), lambda i, j, k: (0, j)))
        else:
            bpt = rows_per_batch // tm
            in_specs.append(pl.BlockSpec((None, 1, tn), lambda i, j, k, bpt=bpt: (i // bpt, 0, j)))
        args.append(arr)
    scratch = [pltpu.VMEM((tm, tn), F32)] if nk > 1 else []
    return pl.pallas_call(
        functools.partial(_mm_body, nk=nk, n_extra=len(extras), epi=epi, a_act=a_act, b_t=b_t),
        grid=(m // tm, n // tn, nk),
        in_specs=in_specs,
        out_specs=pl.BlockSpec((tm, tn), lambda i, j, k: (i, j)),
        out_shape=jax.ShapeDtypeStruct((m, n), out_dtype),
        scratch_shapes=scratch,
        compiler_params=_cparams(("parallel", "parallel", "arbitrary")),
        name=name,
    )(*args)


def _mm_gather_body(a_ref, b_ref, o_ref, scr_ref, *, n2):
    tn = b_ref.shape[0]
    rows = o_ref.shape[0]
    for c0 in range(0, tn, MM_SUB):
        p = lax.dot_general(a_ref[...], b_ref[c0:c0 + MM_SUB, :], (((1,), (1,)), ((), ())),
                            preferred_element_type=F32)
        for r in range(rows):
            scr_ref[:, r, c0:c0 + MM_SUB] = p[r * n2:(r + 1) * n2, :]
    for j in range(n2):
        o_ref[:, j * tn:(j + 1) * tn] = scr_ref[j].astype(o_ref.dtype)


def _mm_gather(a, b_t, n2, name):
    m, kdim = a.shape
    n = b_t.shape[0]
    tm = 16 * n2
    return pl.pallas_call(
        functools.partial(_mm_gather_body, n2=n2),
        grid=(m // tm,),
        in_specs=[pl.BlockSpec((tm, kdim), lambda i: (i, 0)),
                  pl.BlockSpec((n, kdim), lambda i: (0, 0), pipeline_mode=pl.Buffered(1))],
        out_specs=pl.BlockSpec((tm // n2, n2 * n), lambda i: (i, 0)),
        out_shape=jax.ShapeDtypeStruct((m // n2, n2 * n), BF16),
        scratch_shapes=[pltpu.VMEM((n2, tm // n2, n), F32)],
        compiler_params=_cparams(("parallel",)),
        name=name,
    )(a, b_t)


def _cast_rows_body(w_ref, o_ref):
    o_ref[...] = w_ref[...].astype(o_ref.dtype)


def _cast_rows_offset_body(main_ref, next_ref, o_ref):
    half = next_ref.shape[0]
    rb = main_ref.shape[0]
    o_ref[0:rb - half, :] = main_ref[half:rb, :].astype(o_ref.dtype)
    o_ref[rb - half:rb, :] = next_ref[...].astype(o_ref.dtype)


def _cast_rows(w, l, start, n_rows):
    kdim = w.shape[2]
    half = LANE // 2
    if start % LANE == half and n_rows % LANE == 0:
        base = start - half
        rb = 2 * LANE if base % (2 * LANE) == 0 and n_rows % (2 * LANE) == 0 else LANE
        jb, r = base // rb, rb // half
        return pl.pallas_call(
            _cast_rows_offset_body,
            grid=(n_rows // rb,),
            in_specs=[pl.BlockSpec((None, rb, kdim), lambda j: (l, jb + j, 0)),
                      pl.BlockSpec((None, half, kdim), lambda j: (l, (jb + j + 1) * r, 0))],
            out_specs=pl.BlockSpec((rb, kdim), lambda j: (j, 0)),
            out_shape=jax.ShapeDtypeStruct((n_rows, kdim), BF16),
            compiler_params=_cparams(("parallel",)),
            name="cast_rows_offset",
        )(w, w)
    rb = 512
    while start % rb or n_rows % rb:
        rb //= 2
    jb = start // rb
    return pl.pallas_call(
        _cast_rows_body,
        grid=(n_rows // rb,),
        in_specs=[pl.BlockSpec((None, rb, kdim), lambda j: (l, jb + j, 0))],
        out_specs=pl.BlockSpec((rb, kdim), lambda j: (j, 0)),
        out_shape=jax.ShapeDtypeStruct((n_rows, kdim), BF16),
        compiler_params=_cparams(("parallel",)),
        name="cast_rows",
    )(w)


def _norm_body(x_ref, g_ref, scale_ref, shift_ref, o_ref):
    x = x_ref[...]
    r = lax.rsqrt(jnp.mean(x * x, axis=-1, keepdims=True) + EPS)
    y = x * r * g_ref[...]
    o_ref[...] = (y * (1.0 + scale_ref[...]) + shift_ref[...]).astype(o_ref.dtype)


def _norm_mod(x, g, scale, shift):
    b, s, d = x.shape
    tr = _tile(s, 256)
    return pl.pallas_call(
        _norm_body,
        grid=(b, s // tr),
        in_specs=[pl.BlockSpec((None, tr, d), lambda bi, i: (bi, i, 0)),
                  pl.BlockSpec((1, d), lambda bi, i: (0, 0)),
                  pl.BlockSpec((None, 1, d), lambda bi, i: (bi, 0, 0)),
                  pl.BlockSpec((None, 1, d), lambda bi, i: (bi, 0, 0))],
        out_specs=pl.BlockSpec((None, tr, d), lambda bi, i: (bi, i, 0)),
        out_shape=jax.ShapeDtypeStruct((b, s, d), BF16),
        compiler_params=_cparams(("parallel", "parallel")),
        name="norm_mod",
    )(x, g, scale, shift)


def _swap_quarters(v, even_quarter):
    quarter = QK_ROPE // 4
    return jnp.where(even_quarter, pltpu.roll(v, LANE - quarter, axis=1), pltpu.roll(v, quarter, axis=1))


_NT = (((1,), (1,)), ((), ()))


def _qprep_body(z_ref, an_ref, w_ref, gn_ref, gr_ref, gs_ref, cos_ref, sin_ref, q_ref, *, n_heads, scale):
    x = z_ref[...]
    r = lax.rsqrt(jnp.mean(x * x, axis=-1, keepdims=True) + EPS)
    xn = (x * r * an_ref[...]).astype(BF16)
    acc = _dot(xn, w_ref[...])
    cos = cos_ref[...] * gr_ref[...]
    sin = sin_ref[...] * gs_ref[...]
    gn = gn_ref[...]
    hw = n_heads * LANE
    for h in range(n_heads):
        nope = acc[:, h * LANE:(h + 1) * LANE]
        rp = acc[:, hw + h * LANE: hw + (h + 1) * LANE]
        rs = acc[:, 2 * hw + h * LANE: 2 * hw + (h + 1) * LANE]
        ss = jnp.sum(nope * nope + rp * rp, axis=-1, keepdims=True)
        inv = lax.rsqrt(ss * (1.0 / QK_DIM) + EPS) * scale
        q_ref[h, :, 0:LANE] = (nope * inv * gn).astype(q_ref.dtype)
        q_ref[h, :, LANE:HEAD_PAD] = ((rp * cos + rs * sin) * inv).astype(q_ref.dtype)


def _qprep(zq, a_norm, w_p, gn, gr, gs, cos, sin, n_heads):
    b, s, ql = zq.shape
    tr = _tile(s, 512)
    nw = w_p.shape[1]
    return pl.pallas_call(
        functools.partial(_qprep_body, n_heads=n_heads, scale=QK_DIM ** -0.5),
        grid=(b, s // tr),
        in_specs=[pl.BlockSpec((None, tr, ql), lambda bi, i: (bi, i, 0)),
                  pl.BlockSpec((1, ql), lambda bi, i: (0, 0)),
                  pl.BlockSpec((ql, nw), lambda bi, i: (0, 0), pipeline_mode=pl.Buffered(1)),
                  pl.BlockSpec((1, LANE), lambda bi, i: (0, 0)),
                  pl.BlockSpec((1, LANE), lambda bi, i: (0, 0)),
                  pl.BlockSpec((1, LANE), lambda bi, i: (0, 0)),
                  pl.BlockSpec((tr, LANE), lambda bi, i: (i, 0)),
                  pl.BlockSpec((tr, LANE), lambda bi, i: (i, 0))],
        out_specs=pl.BlockSpec((None, n_heads, tr, HEAD_PAD), lambda bi, i: (bi, 0, i, 0)),
        out_shape=jax.ShapeDtypeStruct((b, n_heads, s, HEAD_PAD), BF16),
        compiler_params=_cparams(("parallel", "parallel")),
        name="q_prep",
    )(zq, a_norm, w_p, gn, gr, gs, cos, sin)


def _kvprep_body(h_ref, wkv_ref, an_ref, w_ref, gn_ref, gr_ref, gs_ref, cos_ref, sin_ref, k_ref, v_ref, *,
                 n_heads, kv_lora):
    z = lax.dot_general(h_ref[...], wkv_ref[...], _NT, preferred_element_type=F32)
    x = z[:, 0:kv_lora]
    tr = x.shape[0]
    lane = lax.broadcasted_iota(jnp.int32, (tr, LANE), 1)
    kr = jnp.where(lane < QK_ROPE, z[:, kv_lora:kv_lora + LANE], 0.0)
    krs = _swap_quarters(kr, (lane // (QK_ROPE // 4)) % 2 == 0)
    r = lax.rsqrt(jnp.mean(x * x, axis=-1, keepdims=True) + EPS)
    xn = (x * r * an_ref[...]).astype(BF16)
    acc = _dot(xn, w_ref[...])
    ss_r = jnp.sum(kr * kr, axis=-1, keepdims=True)
    rope = kr * (cos_ref[...] * gr_ref[...]) + krs * (sin_ref[...] * gs_ref[...])
    gn = gn_ref[...]
    hw = n_heads * LANE
    ones_col = (lane == 0).astype(v_ref.dtype)
    for h in range(n_heads):
        kn = acc[:, h * LANE:(h + 1) * LANE]
        ss = jnp.sum(kn * kn, axis=-1, keepdims=True) + ss_r
        inv = lax.rsqrt(ss * (1.0 / QK_DIM) + EPS)
        k_ref[h, :, 0:LANE] = (kn * inv * gn).astype(k_ref.dtype)
        k_ref[h, :, LANE:HEAD_PAD] = (rope * inv).astype(k_ref.dtype)
        v_ref[h, :, 0:LANE] = acc[:, hw + h * LANE: hw + (h + 1) * LANE].astype(v_ref.dtype)
        v_ref[h, :, LANE:HEAD_PAD] = ones_col


def _kvprep(h, w_kv, a_norm, w_p, gn, gr, gs, cos, sin, n_heads, kv_lora):
    b, s, d = h.shape
    zw = w_kv.shape[0]
    tr = _tile(s, 256)
    nw = w_p.shape[1]
    out = jax.ShapeDtypeStruct((b, n_heads, s, HEAD_PAD), BF16)
    ospec = pl.BlockSpec((None, n_heads, tr, HEAD_PAD), lambda bi, i: (bi, 0, i, 0))
    once = dict(pipeline_mode=pl.Buffered(1))
    return pl.pallas_call(
        functools.partial(_kvprep_body, n_heads=n_heads, kv_lora=kv_lora),
        grid=(b, s // tr),
        in_specs=[pl.BlockSpec((None, tr, d), lambda bi, i: (bi, i, 0)),
                  pl.BlockSpec((zw, d), lambda bi, i: (0, 0), **once),
                  pl.BlockSpec((1, kv_lora), lambda bi, i: (0, 0)),
                  pl.BlockSpec((kv_lora, nw), lambda bi, i: (0, 0), **once),
                  pl.BlockSpec((1, LANE), lambda bi, i: (0, 0)),
                  pl.BlockSpec((1, LANE), lambda bi, i: (0, 0)),
                  pl.BlockSpec((1, LANE), lambda bi, i: (0, 0)),
                  pl.BlockSpec((tr, LANE), lambda bi, i: (i, 0)),
                  pl.BlockSpec((tr, LANE), lambda bi, i: (i, 0))],
        out_specs=[ospec, ospec],
        out_shape=[out, out],
        compiler_params=_cparams(("parallel", "parallel")),
        name="kv_prep",
    )(h, w_kv, a_norm, w_p, gn, gr, gs, cos, sin)


def _attn_body(*refs, n_src, chunks):
    q_ref = refs[0]
    kv = refs[1:1 + 2 * n_src]
    gate_ref = refs[1 + 2 * n_src]
    o_ref = refs[2 + 2 * n_src]
    q = q_ref[...]
    tq = q.shape[0]
    plan = []
    for si in range(n_src):
        tk, n_chunks = chunks[si]
        plan += [(kv[2 * si], kv[2 * si + 1], ci * tk, tk) for ci in range(n_chunks)]

    def scores(i):
        k_ref, _, start, tk = plan[i]
        return lax.dot_general(q, k_ref[start:start + tk, :], (((1,), (1,)), ((), ())),
                               preferred_element_type=F32)

    m = jnp.full((tq, LANE), -jnp.inf, F32)
    acc = jnp.zeros((tq, HEAD_PAD), F32)
    s_next = scores(0)
    for i in range(len(plan)):
        s = s_next
        if i + 1 < len(plan):
            s_next = scores(i + 1)
        _, v_ref, start, tk = plan[i]
        m_new = jnp.maximum(m, jnp.max(s, axis=-1, keepdims=True))
        p = jnp.exp((s - jnp.tile(m_new, (1, tk // LANE))).astype(BF16))
        alpha = jnp.exp(m - m_new)
        acc = acc * jnp.tile(alpha, (1, HEAD_PAD // LANE)) + _dot(p, v_ref[start:start + tk, :])
        m = m_new
    out = acc[:, 0:V_DIM] / acc[:, V_DIM:V_DIM + 1]
    o_ref[...] = (out * gate_ref[...].astype(F32)).astype(o_ref.dtype)


def _attention(q, sources, gates, gate_col, n_heads):
    b, _, s, _ = q.shape
    tq = _tile(s, ATTN_TQ)
    in_specs = [pl.BlockSpec((None, None, tq, HEAD_PAD), lambda bi, h, i: (bi, h, i, 0))]
    args = [q]
    chunks = []
    for k, v in sources:
        sk = k.shape[2]
        tk = _tile(sk, ATTN_TK)
        chunks.append((tk, sk // tk))
        spec = pl.BlockSpec((None, None, sk, HEAD_PAD), lambda bi, h, i: (bi, h, 0, 0))
        in_specs += [spec, spec]
        args += [k, v]
    gb = gate_col // V_DIM
    in_specs.append(pl.BlockSpec((None, tq, V_DIM), lambda bi, h, i: (bi, i, gb + h)))
    args.append(gates)
    return pl.pallas_call(
        functools.partial(_attn_body, n_src=len(sources), chunks=tuple(chunks)),
        grid=(b, n_heads, s // tq),
        in_specs=in_specs,
        out_specs=pl.BlockSpec((None, tq, V_DIM), lambda bi, h, i: (bi, i, h)),
        out_shape=jax.ShapeDtypeStruct((b, s, n_heads * V_DIM), BF16),
        compiler_params=_cparams(("parallel", "parallel", "arbitrary")),
        name="attention",
    )(*args)


def _fft_factors(s):
    n1 = 128 if s % 128 == 0 and s // 128 >= 16 else 16
    return n1, s // n1


@functools.lru_cache(maxsize=None)
def _fft_tables(s, c):
    n1, n2 = _fft_factors(s)
    k1 = np.arange(n1, dtype=np.float64)
    a1 = 2.0 * np.pi * np.outer(k1, k1) / n1
    t1 = np.concatenate([np.cos(a1), -np.sin(a1)], axis=0)
    at = 2.0 * np.pi * np.outer(np.arange(n2, dtype=np.float64), k1) / s
    twr = np.cos(at)[:, :, None]
    twi = (-np.sin(at))[:, :, None]
    k2 = np.arange(n2, dtype=np.float64)
    a3 = 2.0 * np.pi * np.outer(k2, k2) / n2
    c3, s3 = np.cos(a3), np.sin(a3)
    t3 = np.block([[c3, s3], [-s3, c3]])
    kc = np.arange(c, dtype=np.float64)
    ac = 2.0 * np.pi * np.outer(kc, kc) / c
    tc = np.stack([np.cos(ac), np.sin(ac)], axis=0) / math.sqrt(float(s) * float(c))

    bf = lambda t: t.astype(np.float32).astype(BF16)
    return dict(t1=bf(t1), twr=twr.astype(np.float32), twi=twi.astype(np.float32), t3=bf(t3), tc=bf(tc))


def _fft1_body(u_ref, t_ref, twr_ref, twi_ref, y_ref, s_ref, *, n1, jt):
    c = s_ref.shape[-1]
    for j in range(jt):
        y = _dot(t_ref[...], u_ref[:, j * c:(j + 1) * c])
        yr, yi = y[0:n1], y[n1:2 * n1]
        tr, ti = twr_ref[j], twi_ref[j]
        s_ref[0, :, j, :] = yr * tr - yi * ti
        s_ref[1, :, j, :] = yr * ti + yi * tr
    for k in range(n1):
        y_ref[0, :, k * c:(k + 1) * c] = s_ref[0, k].astype(y_ref.dtype)
        y_ref[1, :, k * c:(k + 1) * c] = s_ref[1, k].astype(y_ref.dtype)


def _fft3_body(y_ref, t_ref, x_ref):
    x_ref[...] = _dot(t_ref[...], y_ref[...]).astype(x_ref.dtype)


def _fftc_body(x_ref, t_ref, wf_ref, gate_ref, o_ref, scr_ref, *, n2, kt):
    c = wf_ref.shape[0]
    xr = jnp.concatenate([x_ref[0:n2, j * c:(j + 1) * c] for j in range(kt)], axis=0)
    xi = jnp.concatenate([x_ref[n2:2 * n2, j * c:(j + 1) * c] for j in range(kt)], axis=0)
    z = _dot(xr, t_ref[0]) + _dot(xi, t_ref[1])
    y = _dot(z.astype(BF16), wf_ref[...])
    for j in range(kt):
        scr_ref[:, j, :] = y[j * n2:(j + 1) * n2, :]
    o_ref[...] = (scr_ref[...] * gate_ref[...].astype(F32)).astype(o_ref.dtype)


def _fourier_branch(u, w_fnet, gates, gate_col):
    b, s, c = u.shape
    n1, n2 = _fft_factors(s)
    tabs = _fft_tables(s, c)
    t1, t3, tc = (jnp.asarray(tabs[k]) for k in ("t1", "t3", "tc"))
    twr, twi = jnp.asarray(tabs["twr"]), jnp.asarray(tabs["twi"])

    const2 = lambda bi, j: (0, 0)
    jt = min(16, n2)
    y = pl.pallas_call(
        functools.partial(_fft1_body, n1=n1, jt=jt),
        grid=(b, n2 // jt),
        in_specs=[pl.BlockSpec((None, n1, jt * c), lambda bi, j: (bi, 0, j)),
                  pl.BlockSpec((2 * n1, n1), const2),
                  pl.BlockSpec((jt, n1, 1), lambda bi, j: (j, 0, 0)),
                  pl.BlockSpec((jt, n1, 1), lambda bi, j: (j, 0, 0))],
        out_specs=pl.BlockSpec((None, 2, jt, n1 * c), lambda bi, j: (bi, 0, j, 0)),
        out_shape=jax.ShapeDtypeStruct((b, 2, n2, n1 * c), BF16),
        scratch_shapes=[pltpu.VMEM((2, n1, jt, c), F32)],
        compiler_params=_cparams(("parallel", "parallel")),
        name="fft_stage1",
    )(u.reshape(b, n1, n2 * c), t1, twr, twi)

    cols = n1 * c
    tn = _tile(cols, 8192, LANE)
    x = pl.pallas_call(
        _fft3_body,
        grid=(b, cols // tn),
        in_specs=[pl.BlockSpec((None, 2 * n2, tn), lambda bi, j: (bi, 0, j)),
                  pl.BlockSpec((2 * n2, 2 * n2), const2)],
        out_specs=pl.BlockSpec((None, 2 * n2, tn), lambda bi, j: (bi, 0, j)),
        out_shape=jax.ShapeDtypeStruct((b, 2 * n2, cols), BF16),
        compiler_params=_cparams(("parallel", "parallel")),
        name="fft_stage2",
    )(y.reshape(b, 2 * n2, cols), t3)

    assert gate_col == 0 and gates.shape[-1] == c
    kt = min(16, n1)
    once = dict(pipeline_mode=pl.Buffered(1))
    seq_spec = pl.BlockSpec((None, n2, kt, c), lambda bi, j: (bi, 0, j, 0))
    yf = pl.pallas_call(
        functools.partial(_fftc_body, n2=n2, kt=kt),
        grid=(b, n1 // kt),
        in_specs=[pl.BlockSpec((None, 2 * n2, kt * c), lambda bi, j: (bi, 0, j)),
                  pl.BlockSpec((2, c, c), lambda bi, j: (0, 0, 0), **once),
                  pl.BlockSpec((c, c), lambda bi, j: (0, 0), **once),
                  seq_spec],
        out_specs=seq_spec,
        out_shape=jax.ShapeDtypeStruct((b, n2, n1, c), BF16),
        scratch_shapes=[pltpu.VMEM((n2, kt, c), F32)],
        compiler_params=_cparams(("parallel", "parallel")),
        name="fft_channel",
    )(x, tc, w_fnet, gates.reshape(b, n2, n1, c))
    return yf.reshape(b, s, c)


CONV_GROUP = 256


def _conv_body(hl_ref, hc_ref, hr_ref, wa_ref, wg_ref, cw_ref, cb_ref, lg_ref, lb_ref, w_ref, gate_ref, o_ref,
               ext_ref, sh_ref, acc_ref, *, c, ts, n_tiles):
    i = pl.program_id(1)
    h_ext = jnp.concatenate([hl_ref[...], hc_ref[...], hr_ref[...]], axis=0)
    row = lax.broadcasted_iota(jnp.int32, (ts + 2 * CONV_HALO, 1), 0)
    inside = ((row >= CONV_HALO) | (i > 0)) & ((row < CONV_HALO + ts) | (i < n_tiles - 1))
    span = sh_ref.shape[1]
    off = CONV_HALO - CONV_W // 2
    nt = (((1,), (1,)), ((), ()))
    group = min(CONV_GROUP, c)
    for c0 in range(0, c, group):
        cs = slice(c0, c0 + group)
        za = lax.dot_general(h_ext, wa_ref[cs, :], nt, preferred_element_type=F32)
        zg = lax.dot_general(h_ext, wg_ref[cs, :], nt, preferred_element_type=F32)
        ext_ref[:, cs] = jnp.where(inside, za * _sigmoid(zg), 0.0)
        for r in range(1, SUBLANE):
            sh_ref[r - 1, :, cs] = ext_ref[r:r + span, cs]
        acc = jnp.zeros((ts, group), F32) + cb_ref[:, cs]
        for j in range(CONV_W):
            r, base = (off + j) % SUBLANE, (off + j) // SUBLANE * SUBLANE
            src = ext_ref[base:base + ts, cs] if r == 0 else sh_ref[r - 1, base:base + ts, cs]
            acc = acc + src * cw_ref[j:j + 1, cs]
        acc_ref[:, cs] = acc
    acc = acc_ref[...]
    mu = jnp.mean(acc, axis=-1, keepdims=True)
    d = acc - mu
    var = jnp.mean(d * d, axis=-1, keepdims=True)
    y = _silu(d * lax.rsqrt(var + EPS) * lg_ref[...] + lb_ref[...])
    out = _dot(y.astype(BF16), w_ref[...])
    o_ref[...] = (out * gate_ref[...].astype(F32)).astype(o_ref.dtype)


def _conv_branch(h, w_t, glu_rows, conv_w, conv_b, cln_g, cln_b, w_pw2, gates):
    b, s, d = h.shape
    start, c2 = glu_rows
    c = c2 // 2
    ts = _tile(s, 256, CONV_HALO)
    n_tiles = s // ts
    hb = ts // CONV_HALO
    n_hblk = s // CONV_HALO
    row = lambda bi, i: (0, 0)
    once = dict(pipeline_mode=pl.Buffered(1))
    ja = start // c
    return pl.pallas_call(
        functools.partial(_conv_body, c=c, ts=ts, n_tiles=n_tiles),
        grid=(b, n_tiles),
        in_specs=[pl.BlockSpec((None, CONV_HALO, d), lambda bi, i: (bi, jnp.maximum(i * hb - 1, 0), 0)),
                  pl.BlockSpec((None, ts, d), lambda bi, i: (bi, i, 0)),
                  pl.BlockSpec((None, CONV_HALO, d),
                               lambda bi, i: (bi, jnp.minimum((i + 1) * hb, n_hblk - 1), 0)),
                  pl.BlockSpec((c, d), lambda bi, i: (ja, 0), **once),
                  pl.BlockSpec((c, d), lambda bi, i: (ja + 1, 0), **once),
                  pl.BlockSpec((CONV_W, c), row),
                  pl.BlockSpec((1, c), row),
                  pl.BlockSpec((1, c), row),
                  pl.BlockSpec((1, c), row),
                  pl.BlockSpec((c, c), row, **once),
                  pl.BlockSpec((None, ts, c), lambda bi, i: (bi, i, 0))],
        out_specs=pl.BlockSpec((None, ts, c), lambda bi, i: (bi, i, 0)),
        out_shape=jax.ShapeDtypeStruct((b, s, c), BF16),
        scratch_shapes=[pltpu.VMEM((ts + 2 * CONV_HALO, c), F32),
                        pltpu.VMEM((SUBLANE - 1, ts + 2 * CONV_HALO - SUBLANE, c), F32),
                        pltpu.VMEM((ts, c), F32)],
        compiler_params=_cparams(("parallel", "arbitrary")),
        name="conv_branch",
    )(h, h, h, w_t, w_t, conv_w, conv_b, cln_g, cln_b, w_pw2, gates)


def _merge_body(yf_ref, ym_ref, yc_ref, wf_ref, wm_ref, wc_ref, gf_ref, gm_ref, gc_ref, o_ref):
    yf, ym, yc = yf_ref[...], ym_ref[...], yc_ref[...]
    tn = o_ref.shape[-1]
    sub = MM_SUB if tn % MM_SUB == 0 else tn
    for c0 in range(0, tn, sub):
        cs = slice(c0, c0 + sub)
        acc = gf_ref[:, cs].astype(F32) * _dot(yf, wf_ref[:, cs])
        acc = acc + gm_ref[:, cs].astype(F32) * _dot(ym, wm_ref[:, cs])
        acc = acc + gc_ref[:, cs].astype(F32) * _dot(yc, wc_ref[:, cs])
        o_ref[:, cs] = acc.astype(o_ref.dtype)


def _merge(yf, ym, yc, wf, wm, wc, g):
    m = yf.shape[0]
    d = wf.shape[1]
    tm = _tile(m, 512)
    tn = _tile(d, 1024, LANE)
    nj = d // tn
    yspec = lambda arr: pl.BlockSpec((tm, arr.shape[1]), lambda j, i: (i, 0))
    wspec = lambda arr: pl.BlockSpec((arr.shape[0], tn), lambda j, i: (0, j))
    gspec = lambda t: pl.BlockSpec((tm, tn), lambda j, i, t=t: (i, t * nj + j))
    return pl.pallas_call(
        _merge_body,
        grid=(nj, m // tm),
        in_specs=[yspec(yf), yspec(ym), yspec(yc), wspec(wf), wspec(wm), wspec(wc),
                  gspec(0), gspec(1), gspec(2)],
        out_specs=pl.BlockSpec((tm, tn), lambda j, i: (i, j)),
        out_shape=jax.ShapeDtypeStruct((m, d), BF16),
        compiler_params=_cparams(("parallel", "parallel")),
        name="merge",
    )(yf, ym, yc, wf, wm, wc, g, g, g)


def _rope_tables(n_tok):
    half = QK_ROPE // 4
    inv = ROPE_BASE ** (-jnp.arange(half, dtype=F32) / half)
    t = jnp.arange(n_tok, dtype=jnp.int32)
    rows = (t // GRID_W).astype(F32)[:, None] * inv[None, :]
    cols = (t % GRID_W).astype(F32)[:, None] * inv[None, :]
    cr, sr, cc, sc = jnp.cos(rows), jnp.sin(rows), jnp.cos(cols), jnp.sin(cols)
    pad1 = jnp.ones((n_tok, LANE - QK_ROPE), F32)
    pad0 = jnp.zeros((n_tok, LANE - QK_ROPE), F32)
    cos = jnp.concatenate([cr, cr, cc, cc, pad1], axis=1)
    sin = jnp.concatenate([-sr, sr, -sc, sc, pad0], axis=1)
    return cos, sin


def _swap_perm():
    q = QK_ROPE // 4
    return np.concatenate([np.arange(q, 2 * q), np.arange(0, q), np.arange(3 * q, 4 * q), np.arange(2 * q, 3 * q)])


def _pad_lanes(w, width=LANE):
    return jnp.pad(w, [(0, 0)] * (w.ndim - 1) + [(0, width - w.shape[-1])])


def _prep_layer(l, dims, w_in, w_uq, w_ukv, q_norm, k_norm, w_fnet, w_pw2, w_br_f, w_br_m, w_br_c, w_out):
    d, f, ql, kvl, h, c = dims
    mla = h * V_DIM
    off_fg = f
    off_q = 2 * f
    off_kv = off_q + ql
    off_kr = off_kv + kvl
    off_mg = off_kr + QK_ROPE
    off_glu = off_mg + mla
    off_cg = off_glu + 2 * c
    off_merge = off_cg + c
    perm = _swap_perm()
    w_u = _cast_rows(w_in, l, 0, f)
    w_fg = _cast_rows(w_in, l, off_fg, f)
    w_q = _cast_rows(w_in, l, off_q, ql)
    w_kv = _cast_rows(w_in, l, off_kv, kvl + LANE)
    n_in = w_in.shape[1]
    w_tail = _cast_rows(w_in, l, off_mg, n_in - off_mg)
    tail = dict(mg=(0, mla), glu=(off_glu - off_mg, 2 * c), cg=(off_cg - off_mg, c),
                merge=(off_merge - off_mg, n_in - off_merge))

    wq3 = w_uq[l].reshape(ql, h, QK_DIM)
    q_nope = wq3[:, :, :QK_NOPE].reshape(ql, h * LANE)
    q_rope = wq3[:, :, QK_NOPE:]
    w_uq_p = jnp.concatenate([q_nope, _pad_lanes(q_rope).reshape(ql, h * LANE),
                              _pad_lanes(q_rope[:, :, perm]).reshape(ql, h * LANE)], axis=1).astype(BF16)
    wkv3 = w_ukv[l].reshape(kvl, h, QK_NOPE + V_DIM)
    w_ukv_p = jnp.concatenate([wkv3[:, :, :QK_NOPE].reshape(kvl, h * LANE),
                               wkv3[:, :, QK_NOPE:].reshape(kvl, h * LANE)], axis=1).astype(BF16)

    def gains(g):
        return (g[None, :QK_NOPE], _pad_lanes(g[None, QK_NOPE:]), _pad_lanes(g[None, QK_NOPE:][:, perm]))

    return dict(w_u=w_u, w_fg=w_fg, w_q=w_q, w_kv=w_kv, w_tail=w_tail, tail=tail,
                w_uq=w_uq_p, w_ukv=w_ukv_p, qg=gains(q_norm[l]), kg=gains(k_norm[l]),
                w_fnet=w_fnet[l].astype(BF16), w_pw2=w_pw2[l].astype(BF16),
                w_br_f=w_br_f[l].astype(BF16), w_br_m=w_br_m[l].astype(BF16), w_br_c=w_br_c[l].astype(BF16),
                w_out=w_out[l].astype(BF16))


def _kv_stream(h, w, kv_a_norm, cos, sin, n_heads, kv_lora):
    gn, gr, gs = w["kg"]
    return _kvprep(h, w["w_kv"], kv_a_norm, w["w_ukv"], gn, gr, gs, cos, sin, n_heads, kv_lora)


def _full_stream(x, h, kv_own, kv_ctx, gate, w, p, cos, sin, dims):
    d, f, ql, kvl, n_heads, c = dims
    bsz, s, _ = x.shape
    m = bsz * s
    h2 = h.reshape(m, d)
    wt, tail = w["w_tail"], w["tail"]
    nt = dict(b_t=True)
    zu = _mm_gather(h2, w["w_u"], _fft_factors(s)[1], "in_proj_u")
    gate_f = _mm(h2, w["w_fg"], tn=1024, out_dtype=BF16, epi=_silu, name="in_proj_gate_f", **nt)
    zq = _mm(h2, w["w_q"], tn=ql, out_dtype=F32, name="in_proj_q", **nt)
    gate_m = _mm(h2, wt, b_cols=tail["mg"], tn=1024, out_dtype=BF16, epi=_silu, name="in_proj_gate_m", **nt)
    gate_c = _mm(h2, wt, b_cols=tail["cg"], tn=1024, out_dtype=BF16, epi=_silu, name="in_proj_gate_c", **nt)
    gmerge = _mm(h2, wt, b_cols=tail["merge"], tn=1024, out_dtype=BF16, epi=_sigmoid, name="in_proj_merge",
                 **nt)

    gn, gr, gs = w["qg"]
    q = _qprep(zq.reshape(bsz, s, ql), p["q_a_norm"], w["w_uq"], gn, gr, gs, cos, sin, n_heads)
    sources = ([kv_ctx] if kv_ctx is not None else []) + [kv_own]
    y_m = _attention(q, sources, gate_m.reshape(bsz, s, -1), 0, n_heads)
    y_f = _fourier_branch(zu.reshape(bsz, s, f), w["w_fnet"], gate_f.reshape(bsz, s, f), 0)
    y_c = _conv_branch(h, wt, tail["glu"], p["conv_w"], p["conv_b"], p["cln_g"], p["cln_b"],
                       w["w_pw2"], gate_c.reshape(bsz, s, c))
    merged = _merge(y_f.reshape(m, f), y_m.reshape(m, -1), y_c.reshape(m, c),
                    w["w_br_f"], w["w_br_m"], w["w_br_c"], gmerge)
    out = _mm(merged, w["w_out"], out_dtype=F32, rows_per_batch=s,
              epi=lambda acc, xv, gv: xv + gv * acc,
              extras=(("mn", x.reshape(m, d)), ("bn", gate)), name="out_proj")
    return out.reshape(bsz, s, d)


def kernel(x, c, ctx, c_ctx, norm_g, w_ada, b_ada, w_in, q_a_norm, w_uq, kv_a_norm, w_ukv, q_norm, k_norm,
           w_fnet, conv_w, conv_b, cln_g, cln_b, w_pw2, w_br_f, w_br_m, w_br_c, w_out):
    bsz, n_tok, d = x.shape
    n_ctx = ctx.shape[1]
    depth = w_in.shape[0]
    f = w_fnet.shape[1]
    ql = q_a_norm.shape[1]
    kvl = kv_a_norm.shape[1]
    n_heads = w_uq.shape[2] // QK_DIM
    cdim = conv_b.shape[1]
    dims = (d, f, ql, kvl, n_heads, cdim)

    cos_l, sin_l = _rope_tables(n_tok)
    cos_c = jnp.ones((n_ctx, LANE), F32)
    sin_c = jnp.zeros((n_ctx, LANE), F32)

    n_rows = -(-(bsz + 1) // 8) * 8
    cond = jnp.concatenate([c, c_ctx[None, :], jnp.zeros((n_rows - bsz - 1, d), F32)], axis=0)

    w_in_t = jnp.swapaxes(w_in, 1, 2)

    xl, xc = x, ctx
    for l in range(depth):
        last = l == depth - 1
        w = _prep_layer(l, dims, w_in_t, w_uq, w_ukv, q_norm, k_norm, w_fnet, w_pw2, w_br_f, w_br_m, w_br_c, w_out)
        p = dict(q_a_norm=q_a_norm[l][None], conv_w=conv_w[l], conv_b=conv_b[l][None],
                 cln_g=cln_g[l][None], cln_b=cln_b[l][None])
        mod = _mm(cond, w_ada, b_layer=l, out_dtype=F32, tm=n_rows, tn=1024, tk=2048, a_act=_silu,
                  epi=lambda acc, bias: acc + bias, extras=(("n", b_ada[l][None]),), name="adaln")
        shift_l, scale_l, gate_l = (mod[:bsz, i * d:(i + 1) * d][:, None, :] for i in range(3))
        shift_c, scale_c, gate_c = (jnp.broadcast_to(mod[bsz, i * d:(i + 1) * d][None, None, :], (bsz, 1, d))
                                    for i in range(3))
        g = norm_g[l][None]
        hl = _norm_mod(xl, g, scale_l, shift_l)
        hc = _norm_mod(xc, g, scale_c, shift_c)
        kv_c = _kv_stream(hc, w, kv_a_norm[l][None], cos_c, sin_c, n_heads, kvl)
        kv_l = _kv_stream(hl, w, kv_a_norm[l][None], cos_l, sin_l, n_heads, kvl)
        new_xl = _full_stream(xl, hl, kv_l, kv_c, gate_l, w, p, cos_l, sin_l, dims)
        if not last:
            xc = _full_stream(xc, hc, kv_c, None, gate_c, w, p, cos_c, sin_c, dims)
        xl = new_xl
    return xl
```
